```python
import jax, jax.numpy as jnp
from jax import lax
import numpy as np

D_MODEL = 2048
BATCH = 8
SEQ = 4096
DEPTH = 4

N_MEM = 256
N_MIXERS = 3
N_A_LAYERS = (DEPTH + 2) // 3
N_B_LAYERS = (DEPTH + 1) // 3
N_C_LAYERS = DEPTH // 3
SHORT_CONV = 3
CHUNK = 128
GMLP_GROUPS = 8
GMLP_HIDDEN = D_MODEL
GMLP_GROUP_DIM = GMLP_HIDDEN // GMLP_GROUPS
CONF_CONV = 31
XA_HEADS = 4
XA_HEAD_DIM = D_MODEL // XA_HEADS
D_FF = ((8 * D_MODEL + 3 * 256 - 1) // (3 * 256)) * 256
EPS = 1e-6

kernel_name = 'hybrid_interleaved_conv_gmlp_conformer_xattn'


def _rmsnorm(x, g):
    xf = x.astype(jnp.float32)
    y = xf * lax.rsqrt(jnp.mean(xf * xf, axis=-1, keepdims=True) + EPS)
    return (y * g.astype(jnp.float32)).astype(x.dtype)


def _layernorm(x, g, b):
    xf = x.astype(jnp.float32)
    mu = jnp.mean(xf, axis=-1, keepdims=True)
    var = jnp.mean(jnp.square(xf - mu), axis=-1, keepdims=True)
    y = (xf - mu) * lax.rsqrt(var + EPS)
    return (y * g.astype(jnp.float32) + b.astype(jnp.float32)).astype(x.dtype)


def _causal_dwconv(x, w):
    k = w.shape[0]
    return lax.conv_general_dilated(
        x, w[:, None, :].astype(x.dtype), window_strides=(1,),
        padding=[(k - 1, 0)], dimension_numbers=('NWC', 'WIO', 'NWC'),
        feature_group_count=x.shape[-1])


def _mixer_short_conv(h, w_in, conv_w, w_out):
    bcz = h @ w_in
    b_gate, c_gate, z = jnp.split(bcz, 3, axis=-1)
    y = _causal_dwconv(c_gate * z, conv_w)
    return (b_gate * y) @ w_out


def _mixer_chunked_gmlp(h, w_in, v_g, v_b, w_s, s_bias, w_out):
    bsz, seq, _ = h.shape
    uv = jax.nn.gelu(h @ w_in)
    u, v = jnp.split(uv, 2, axis=-1)
    v = _layernorm(v, v_g, v_b)
    v = v.reshape(bsz, seq // CHUNK, CHUNK, GMLP_GROUPS, GMLP_GROUP_DIM)
    mask = jnp.tril(jnp.ones((CHUNK, CHUNK), dtype=bool))
    ws = jnp.where(mask[None], w_s, jnp.zeros((), w_s.dtype))
    sv = jnp.einsum('gts,bnsgc->bntgc', ws, v)
    sv = sv + s_bias.T[None, None, :, :, None]
    gated = u * sv.reshape(bsz, seq, GMLP_HIDDEN)
    return gated @ w_out


def _mixer_conformer_conv(h, w_in, conv_w, conv_b, ln_g, ln_b, w_out):
    ag = h @ w_in
    a, g = jnp.split(ag, 2, axis=-1)
    y = a * jax.nn.sigmoid(g)
    y = _causal_dwconv(y, conv_w) + conv_b
    y = _layernorm(y, ln_g, ln_b)
    y = jax.nn.silu(y)
    return y @ w_out


def _cross_attention(h, mem_n, wq, wkv, wo):
    bsz, seq, _ = h.shape
    q = (h @ wq).reshape(bsz, seq, XA_HEADS, XA_HEAD_DIM)
    kv = mem_n @ wkv
    k, v = jnp.split(kv, 2, axis=-1)
    k = k.reshape(bsz, N_MEM, XA_HEADS, XA_HEAD_DIM)
    v = v.reshape(bsz, N_MEM, XA_HEADS, XA_HEAD_DIM)
    scale = XA_HEAD_DIM ** -0.5
    s = jnp.einsum('bshd,bmhd->bhsm', q, k).astype(jnp.float32) * scale
    p = jax.nn.softmax(s, axis=-1).astype(v.dtype)
    o = jnp.einsum('bhsm,bmhd->bshd', p, v).reshape(bsz, seq, D_MODEL)
    return o @ wo


def _swiglu(h, w_gu, w_down):
    gu = h @ w_gu
    gate, up = jnp.split(gu, 2, axis=-1)
    return (jax.nn.silu(gate) * up) @ w_down


def setup_inputs(seed: int = 0) -> dict:
    key = jax.random.key(seed)
    ks = jax.random.split(key, 32)

    def nrm(k, shape, scale):
        return jax.random.normal(k, shape, jnp.float32) * scale

    def gain(k, shape):
        return 1.0 + 0.05 * jax.random.normal(k, shape, jnp.float32)

    d = D_MODEL
    return {
        'x': nrm(ks[0], (BATCH, SEQ, d), 1.0),
        'mem': nrm(ks[1], (BATCH, N_MEM, d), 1.0),
        'mix_norm': gain(ks[2], (DEPTH, 2, d)),
        'xa_norm': gain(ks[3], (DEPTH, 3, d)),
        'xa_wq': nrm(ks[4], (DEPTH, d, d), d ** -0.5),
        'xa_wkv': nrm(ks[5], (DEPTH, d, 2 * d), d ** -0.5),
        'xa_wo': nrm(ks[6], (DEPTH, d, d), d ** -0.5),
        'ffn_norm': gain(ks[7], (DEPTH, 2, d)),
        'ffn_w_gu': nrm(ks[8], (DEPTH, d, 2 * D_FF), d ** -0.5),
        'ffn_w_down': nrm(ks[9], (DEPTH, D_FF, d), D_FF ** -0.5),
        'a_w_in': nrm(ks[10], (N_A_LAYERS, d, 3 * d), d ** -0.5),
        'a_conv_w': nrm(ks[11], (N_A_LAYERS, SHORT_CONV, d), SHORT_CONV ** -0.5),
        'a_w_out': nrm(ks[12], (N_A_LAYERS, d, d), d ** -0.5),
        'b_w_in': nrm(ks[13], (N_B_LAYERS, d, 2 * GMLP_HIDDEN), d ** -0.5),
        'b_v_g': gain(ks[14], (N_B_LAYERS, GMLP_HIDDEN)),
        'b_v_b': nrm(ks[15], (N_B_LAYERS, GMLP_HIDDEN), 0.02),
        'b_w_s': nrm(ks[16], (N_B_LAYERS, GMLP_GROUPS, CHUNK, CHUNK), CHUNK ** -0.5),
        'b_s_bias': gain(ks[17], (N_B_LAYERS, GMLP_GROUPS, CHUNK)),
        'b_w_out': nrm(ks[18], (N_B_LAYERS, GMLP_HIDDEN, d), GMLP_HIDDEN ** -0.5),
        'c_w_in': nrm(ks[19], (N_C_LAYERS, d, 2 * d), d ** -0.5),
        'c_conv_w': nrm(ks[20], (N_C_LAYERS, CONF_CONV, d), CONF_CONV ** -0.5),
        'c_conv_b': nrm(ks[21], (N_C_LAYERS, d), 0.02),
        'c_ln_g': gain(ks[22], (N_C_LAYERS, d)),
        'c_ln_b': nrm(ks[23], (N_C_LAYERS, d), 0.02),
        'c_w_out': nrm(ks[24], (N_C_LAYERS, d, d), d ** -0.5),
    }


def reference(x, mem, mix_norm, xa_norm, xa_wq, xa_wkv, xa_wo, ffn_norm,
              ffn_w_gu, ffn_w_down, a_w_in, a_conv_w, a_w_out,
              b_w_in, b_v_g, b_v_b, b_w_s, b_s_bias, b_w_out,
              c_w_in, c_conv_w, c_conv_b, c_ln_g, c_ln_b, c_w_out):
    for i in range(DEPTH):
        kind = i % N_MIXERS
        slot = i // N_MIXERS
        h = _rmsnorm(x, mix_norm[i, 0])
        if kind == 0:
            y = _mixer_short_conv(h, a_w_in[slot], a_conv_w[slot], a_w_out[slot])
        elif kind == 1:
            y = _mixer_chunked_gmlp(h, b_w_in[slot], b_v_g[slot], b_v_b[slot],
                                    b_w_s[slot], b_s_bias[slot], b_w_out[slot])
        else:
            y = _mixer_conformer_conv(h, c_w_in[slot], c_conv_w[slot], c_conv_b[slot],
                                      c_ln_g[slot], c_ln_b[slot], c_w_out[slot])
        x = x + _rmsnorm(y, mix_norm[i, 1])
        h = _rmsnorm(x, xa_norm[i, 0])
        mem_n = _rmsnorm(mem, xa_norm[i, 2])
        y = _cross_attention(h, mem_n, xa_wq[i], xa_wkv[i], xa_wo[i])
        x = x + _rmsnorm(y, xa_norm[i, 1])
        h = _rmsnorm(x, ffn_norm[i, 0])
        y = _swiglu(h, ffn_w_gu[i], ffn_w_down[i])
        x = x + _rmsnorm(y, ffn_norm[i, 1])
    return x
```

```python
import functools

import jax
import jax.numpy as jnp
from jax import lax
from jax.experimental import pallas as pl
from jax.experimental.pallas import tpu as pltpu

DEPTH = 4
N_MIXERS = 3
N_MEM = 256
SHORT_CONV = 3
CHUNK = 128
GMLP_GROUPS = 8
CONF_CONV = 31
XA_HEADS = 4
EPS = 1e-6

F32 = jnp.float32
BF16 = jnp.bfloat16

ROW_TILE = 256
PROJ_ROWS = 1024
PROJ_COLS = 512
VMEM_LIMIT_BYTES = 56 * 1024 * 1024
SHORT_HALO = 16
CONF_HALO = 32


def _params():
    return pltpu.CompilerParams(vmem_limit_bytes=VMEM_LIMIT_BYTES)


def _resident(shape):
    return pl.BlockSpec(shape, lambda *_: (0,) * len(shape), pipeline_mode=pl.Buffered(1))


def _rms(v, g):
    ms = jnp.mean(v * v, axis=-1, keepdims=True)
    return (v * lax.rsqrt(ms + EPS)) * g


def _layernorm(v, g, b):
    mu = jnp.mean(v, axis=-1, keepdims=True)
    d = v - mu
    var = jnp.mean(d * d, axis=-1, keepdims=True)
    return (d * lax.rsqrt(var + EPS)) * g + b


def _sigmoid(v):
    return 1.0 / (1.0 + jnp.exp(-v))


def _gelu_tanh(v):
    c = 0.7978845608028654
    return v * (0.5 * (1.0 + jnp.tanh(c * (v + 0.044715 * (v * v * v)))))


def _dot(a, b):
    return jnp.dot(a, b, preferred_element_type=F32)


def _finish(y, x_ref, gpost_ref, gnext_ref, xo_ref, ho_ref):
    xn = x_ref[...] + _rms(y, gpost_ref[...])
    xo_ref[...] = xn
    if ho_ref is not None:
        ho_ref[...] = _rms(xn, gnext_ref[...]).astype(BF16)


def _prenorm_body(x_ref, g_ref, h_ref):
    h_ref[...] = _rms(x_ref[...], g_ref[...]).astype(BF16)


def _prenorm(x, g):
    t, d = x.shape
    tm = ROW_TILE
    return pl.pallas_call(
        _prenorm_body,
        grid=(t // tm,),
        in_specs=[pl.BlockSpec((tm, d), lambda i: (i, 0)), _resident((1, d))],
        out_specs=pl.BlockSpec((tm, d), lambda i: (i, 0)),
        out_shape=jax.ShapeDtypeStruct((t, d), BF16),
        compiler_params=_params(),
        name="prenorm",
    )(x, g)


def _kv_body(mem_ref, g_ref, w_ref, kv_ref):
    mem_n = _rms(mem_ref[...], g_ref[0]).astype(BF16)
    kv_ref[0] = _dot(mem_n, w_ref[0]).astype(BF16)


def _memory_kv(mem2d, g_mem, wkv):
    rows, d = mem2d.shape
    depth, _, n = wkv.shape
    tm, tn = N_MEM, 1024
    return pl.pallas_call(
        _kv_body,
        grid=(depth, n // tn, rows // tm),
        in_specs=[
            pl.BlockSpec((tm, d), lambda l, j, i: (i, 0)),
            pl.BlockSpec((1, 1, d), lambda l, j, i: (l, 0, 0)),
            pl.BlockSpec((1, d, tn), lambda l, j, i: (l, 0, j)),
        ],
        out_specs=pl.BlockSpec((1, tm, tn), lambda l, j, i: (l, i, j)),
        out_shape=jax.ShapeDtypeStruct((depth, rows, n), BF16),
        compiler_params=_params(),
        name="memory_kv",
    )(mem2d, g_mem, wkv)


def _proj_call(body, h, w, n_parts, n_out_cols, n_outs, name):
    t, d = h.shape
    tm, tn = min(PROJ_ROWS, t), PROJ_COLS
    nj = n_out_cols // tn
    w_specs = [
        pl.BlockSpec((d, tn), functools.partial(lambda j, i, p: (0, p * nj + j), p=p))
        for p in range(n_parts)
    ]
    out_spec = pl.BlockSpec((tm, tn), lambda j, i: (i, j))
    out_shape = jax.ShapeDtypeStruct((t, n_out_cols), BF16)
    return pl.pallas_call(
        body,
        grid=(nj, t // tm),
        in_specs=[pl.BlockSpec((tm, d), lambda j, i: (i, 0))] + w_specs,
        out_specs=[out_spec] * n_outs if n_outs > 1 else out_spec,
        out_shape=[out_shape] * n_outs if n_outs > 1 else out_shape,
        compiler_params=_params(),
        name=name,
    )(h, *([w] * n_parts))


def _short_in_body(h_ref, wb_ref, wc_ref, wz_ref, b_ref, cz_ref):
    h = h_ref[...]
    b_ref[...] = _dot(h, wb_ref[...]).astype(BF16)
    cz_ref[...] = (_dot(h, wc_ref[...]) * _dot(h, wz_ref[...])).astype(BF16)


def _gmlp_in_body(h_ref, w_ref, uv_ref):
    uv_ref[...] = _gelu_tanh(_dot(h_ref[...], w_ref[...])).astype(BF16)


def _conf_in_body(h_ref, wa_ref, wg_ref, y_ref):
    h = h_ref[...]
    y_ref[...] = (_dot(h, wa_ref[...]) * _sigmoid(_dot(h, wg_ref[...]))).astype(BF16)


def _swiglu_in_body(h_ref, wg_ref, wu_ref, a_ref):
    h = h_ref[...]
    g = _dot(h, wg_ref[...])
    a_ref[...] = ((g * _sigmoid(g)) * _dot(h, wu_ref[...])).astype(BF16)


def _row_call(body, row_inputs, halo, const_inputs, x, g_post, g_next, scratch, name):
    t, d = x.shape
    tm = ROW_TILE
    in_specs, args = [], []
    for arr, ncols, cb in row_inputs:
        in_specs.append(pl.BlockSpec((tm, ncols), functools.partial(lambda i, cb: (i, cb), cb=cb)))
        args.append(arr)
    if halo is not None:
        arr, hr = halo
        step = tm // hr
        in_specs.append(pl.BlockSpec((hr, arr.shape[1]), lambda i: (jnp.maximum(i * step - 1, 0), 0)))
        args.append(arr)
    for arr in const_inputs:
        in_specs.append(_resident(arr.shape))
        args.append(arr)
    in_specs.append(pl.BlockSpec((tm, d), lambda i: (i, 0)))
    args.append(x)
    in_specs.append(_resident(g_post.shape))
    args.append(g_post)
    row_spec = pl.BlockSpec((tm, d), lambda i: (i, 0))
    if g_next is not None:
        in_specs.append(_resident(g_next.shape))
        args.append(g_next)
        out_specs = [row_spec, row_spec]
        out_shape = [jax.ShapeDtypeStruct((t, d), F32), jax.ShapeDtypeStruct((t, d), BF16)]
    else:
        out_specs = row_spec
        out_shape = jax.ShapeDtypeStruct((t, d), F32)
    return pl.pallas_call(
        body,
        grid=(t // tm,),
        in_specs=in_specs,
        out_specs=out_specs,
        out_shape=out_shape,
        scratch_shapes=scratch,
        compiler_params=_params(),
        name=name,
    )(*args)


def _split_tail(refs, has_next, n_scratch):
    n_tail = (5 if has_next else 3) + n_scratch
    lead, tail = refs[: len(refs) - n_tail], refs[len(refs) - n_tail:]
    if has_next:
        x_ref, gpost_ref, gnext_ref, xo_ref, ho_ref = tail[:5]
        scr = tail[5:]
    else:
        x_ref, gpost_ref, xo_ref = tail[:3]
        gnext_ref = ho_ref = None
        scr = tail[3:]
    return lead, (x_ref, gpost_ref, gnext_ref, xo_ref, ho_ref), scr


def _short_out_body(*refs, has_next, seq_tiles):
    (b_ref, cz_ref, halo_ref, cw_ref, w_ref), fin, (ext_ref,) = _split_tail(refs, has_next, 1)
    tm = cz_ref.shape[0]
    at_seq_start = pl.program_id(0) % seq_tiles == 0
    halo = halo_ref[...].astype(F32)
    ext_ref[0:SHORT_HALO, :] = jnp.where(at_seq_start, 0.0, halo)
    ext_ref[SHORT_HALO:, :] = cz_ref[...].astype(F32)
    cw = cw_ref[...]
    y = cw[2:3, :] * ext_ref[pl.ds(SHORT_HALO, tm), :]
    y = y + cw[1:2, :] * ext_ref[pl.ds(SHORT_HALO - 1, tm), :]
    y = y + cw[0:1, :] * ext_ref[pl.ds(SHORT_HALO - 2, tm), :]
    lhs = (b_ref[...].astype(F32) * y).astype(BF16)
    _finish(_dot(lhs, w_ref[...]), *fin)


def _gmlp_out_body(*refs, has_next):
    (u_ref, v_ref, lng_ref, lnb_ref, ws_ref, sb_ref, w_ref), fin, (lhs_ref,) = _split_tail(refs, has_next, 1)
    tm, hidden = u_ref.shape
    gdim = hidden // GMLP_GROUPS
    vn = _layernorm(v_ref[...].astype(F32), lng_ref[...], lnb_ref[...]).astype(BF16)
    row = lax.broadcasted_iota(jnp.int32, (CHUNK, CHUNK), 0)
    col = lax.broadcasted_iota(jnp.int32, (CHUNK, CHUNK), 1)
    causal = col <= row
    sb = sb_ref[...]
    for g in range(GMLP_GROUPS):
        ws = jnp.where(causal, ws_ref[g], 0.0).astype(BF16)
        bias = sb[:, g:g + 1]
        cols = slice(g * gdim, (g + 1) * gdim)
        for c in range(tm // CHUNK):
            rows = slice(c * CHUNK, (c + 1) * CHUNK)
            sv = _dot(ws, vn[rows, cols]) + bias
            lhs_ref[rows, cols] = (u_ref[rows, cols].astype(F32) * sv).astype(BF16)
    _finish(_dot(lhs_ref[...], w_ref[...]), *fin)


def _conf_out_body(*refs, has_next, seq_tiles):
    (y_ref, halo_ref, cw_ref, cb_ref, lng_ref, lnb_ref, w_ref), fin, (ext_ref, acc_ref) = _split_tail(
        refs, has_next, 2)
    tm, d = y_ref.shape
    at_seq_start = pl.program_id(0) % seq_tiles == 0
    halo = halo_ref[...].astype(F32)
    ext_ref[0:CONF_HALO, :] = jnp.where(at_seq_start, 0.0, halo)
    ext_ref[CONF_HALO:, :] = y_ref[...].astype(F32)
    rchunk, cchunk = 32, 512
    base = CONF_HALO - (CONF_CONV - 1)
    for cc in range(d // cchunk):
        cols = slice(cc * cchunk, (cc + 1) * cchunk)
        for rc in range(tm // rchunk):
            acc = cw_ref[0:1, cols] * ext_ref[pl.ds(rc * rchunk + base, rchunk), cols]
            for k in range(1, CONF_CONV):
                acc = acc + cw_ref[k:k + 1, cols] * ext_ref[pl.ds(rc * rchunk + base + k, rchunk), cols]
            acc_ref[rc * rchunk:(rc + 1) * rchunk, cols] = acc
    z = _layernorm(acc_ref[...] + cb_ref[...], lng_ref[...], lnb_ref[...])
    lhs = (z * _sigmoid(z)).astype(BF16)
    _finish(_dot(lhs, w_ref[...]), *fin)


def _xattn_body(*refs, has_next):
    (h_ref, kv_ref, wq_ref, wo_ref), fin, (o_ref,) = _split_tail(refs, has_next, 1)
    d = h_ref.shape[1]
    hd = d // XA_HEADS
    scale = hd ** -0.5
    q = _dot(h_ref[...], wq_ref[...]).astype(BF16)
    for hh in range(XA_HEADS):
        cols = slice(hh * hd, (hh + 1) * hd)
        k = kv_ref[0, :, hh * hd:(hh + 1) * hd]
        v = kv_ref[0, :, d + hh * hd:d + (hh + 1) * hd]
        s = lax.dot_general(q[:, cols], k, (((1,), (1,)), ((), ())), preferred_element_type=F32) * scale
        e = jnp.exp(s - jnp.max(s, axis=-1, keepdims=True))
        p = (e / jnp.sum(e, axis=-1, keepdims=True)).astype(BF16)
        o_ref[:, cols] = _dot(p, v).astype(BF16)
    _finish(_dot(o_ref[...], wo_ref[...]), *fin)


def _xattn(h, kv_l, wq, wo, x, g_post, g_next, seq):
    t, d = x.shape
    tm = ROW_TILE
    has_next = g_next is not None
    tiles_per_seq = seq // tm
    kv3 = kv_l.reshape(t // seq, N_MEM, 2 * d)
    row_spec = pl.BlockSpec((tm, d), lambda i: (i, 0))
    in_specs = [
        row_spec,
        pl.BlockSpec((1, N_MEM, 2 * d), lambda i: (i // tiles_per_seq, 0, 0)),
        _resident(wq.shape),
        _resident(wo.shape),
        row_spec,
        _resident(g_post.shape),
    ]
    args = [h, kv3, wq, wo, x, g_post]
    if has_next:
        in_specs.append(_resident(g_next.shape))
        args.append(g_next)
        out_specs = [row_spec, row_spec]
        out_shape = [jax.ShapeDtypeStruct((t, d), F32), jax.ShapeDtypeStruct((t, d), BF16)]
    else:
        out_specs = row_spec
        out_shape = jax.ShapeDtypeStruct((t, d), F32)
    return pl.pallas_call(
        functools.partial(_xattn_body, has_next=has_next),
        grid=(t // tm,),
        in_specs=in_specs,
        out_specs=out_specs,
        out_shape=out_shape,
        scratch_shapes=[pltpu.VMEM((tm, d), BF16)],
        compiler_params=_params(),
        name="xattn",
    )(*args)


def _down_body(*refs, has_next):
    (a_ref, w_ref), fin, _ = _split_tail(refs, has_next, 0)
    _finish(_dot(a_ref[...], w_ref[...]), *fin)


def kernel(x, mem, mix_norm, xa_norm, xa_wq, xa_wkv, xa_wo, ffn_norm, ffn_w_gu, ffn_w_down,
           a_w_in, a_conv_w, a_w_out, b_w_in, b_v_g, b_v_b, b_w_s, b_s_bias, b_w_out,
           c_w_in, c_conv_w, c_conv_b, c_ln_g, c_ln_b, c_w_out):
    bsz, seq, d = x.shape
    t = bsz * seq
    assert seq % ROW_TILE == 0 and t % PROJ_ROWS == 0 and ROW_TILE % CHUNK == 0
    d_ff = ffn_w_down.shape[1]
    seq_tiles = seq // ROW_TILE
    bf = lambda w: w.astype(BF16)
    row = lambda v: v.reshape(1, -1)

    x2 = x.reshape(t, d)
    kv = _memory_kv(mem.reshape(bsz * N_MEM, d), xa_norm[:, 2].reshape(DEPTH, 1, d), bf(xa_wkv))
    h = _prenorm(x2, row(mix_norm[0, 0]))

    for i in range(DEPTH):
        kind, slot = i % N_MIXERS, i // N_MIXERS
        g_post, g_next = row(mix_norm[i, 1]), row(xa_norm[i, 0])
        if kind == 0:
            b, cz = _proj_call(_short_in_body, h, bf(a_w_in[slot]), 3, d, 2, "short_in")
            x2, h = _row_call(
                functools.partial(_short_out_body, has_next=True, seq_tiles=seq_tiles),
                [(b, d, 0), (cz, d, 0)], (cz, SHORT_HALO), [a_conv_w[slot], bf(a_w_out[slot])],
                x2, g_post, g_next, [pltpu.VMEM((ROW_TILE + SHORT_HALO, d), F32)], "short_out")
        elif kind == 1:
            uv = _proj_call(_gmlp_in_body, h, bf(b_w_in[slot]), 1, 2 * d, 1, "gmlp_in")
            x2, h = _row_call(
                functools.partial(_gmlp_out_body, has_next=True),
                [(uv, d, 0), (uv, d, 1)], None,
                [row(b_v_g[slot]), row(b_v_b[slot]), b_w_s[slot], b_s_bias[slot].T, bf(b_w_out[slot])],
                x2, g_post, g_next, [pltpu.VMEM((ROW_TILE, d), BF16)], "gmlp_out")
        else:
            y0 = _proj_call(_conf_in_body, h, bf(c_w_in[slot]), 2, d, 1, "conf_in")
            x2, h = _row_call(
                functools.partial(_conf_out_body, has_next=True, seq_tiles=seq_tiles),
                [(y0, d, 0)], (y0, CONF_HALO),
                [c_conv_w[slot], row(c_conv_b[slot]), row(c_ln_g[slot]), row(c_ln_b[slot]), bf(c_w_out[slot])],
                x2, g_post, g_next,
                [pltpu.VMEM((ROW_TILE + CONF_HALO, d), F32), pltpu.VMEM((ROW_TILE, d), F32)], "conf_out")

        x2, h = _xattn(h, kv[i], bf(xa_wq[i]), bf(xa_wo[i]), x2, row(xa_norm[i, 1]), row(ffn_norm[i, 0]), seq)

        a = _proj_call(_swiglu_in_body, h, bf(ffn_w_gu[i]), 2, d_ff, 1, "swiglu_in")
        last = i == DEPTH - 1
        g_next = None if last else row(mix_norm[i + 1, 0])
        out = _row_call(
            functools.partial(_down_body, has_next=not last),
            [(a, d_ff, 0)], None, [bf(ffn_w_down[i])],
            x2, row(ffn_norm[i, 1]), g_next, [], "swiglu_down")
        x2, h = (out, None) if last else out

    return x2.reshape(bsz, seq, d)
```

```python
import functools

import jax
import jax.numpy as jnp
from jax import lax
from jax.experimental import pallas as pl
from jax.experimental.pallas import tpu as pltpu

DEPTH = 4
N_MIXERS = 3
N_MEM = 256
SHORT_CONV = 3
CHUNK = 128
GMLP_GROUPS = 8
CONF_CONV = 31
XA_HEADS = 4
EPS = 1e-6

F32 = jnp.float32
BF16 = jnp.bfloat16

ROW_TILE = 256
PROJ_ROWS = 1024
PROJ_COLS = 512
VMEM_LIMIT_BYTES = 56 * 1024 * 1024
SHORT_HALO = 16
CONF_HALO = 32
SUBLANES = 8


def _params():
    return pltpu.CompilerParams(vmem_limit_bytes=VMEM_LIMIT_BYTES)


def _resident(arr, *lead):
    shape = arr.shape[len(lead):]
    idx = tuple(lead) + (0,) * len(shape)
    return pl.BlockSpec((None,) * len(lead) + shape, lambda *_: idx, pipeline_mode=pl.Buffered(1))


def _rms(v, g):
    ms = jnp.mean(v * v, axis=-1, keepdims=True)
    return (v * lax.rsqrt(ms + EPS)) * g


def _layernorm(v, g, b):
    mu = jnp.mean(v, axis=-1, keepdims=True)
    d = v - mu
    var = jnp.mean(d * d, axis=-1, keepdims=True)
    return (d * lax.rsqrt(var + EPS)) * g + b


def _sigmoid(v):
    return 0.5 * jnp.tanh(0.5 * v) + 0.5


def _gelu_tanh(v):
    c = 0.7978845608028654
    return v * (0.5 * (1.0 + jnp.tanh(c * (v + 0.044715 * (v * v * v)))))


def _dot(a, b):
    return jnp.dot(a, b, preferred_element_type=F32)


def _finish(y, x_ref, gpost_ref, gnext_ref, xo_ref, ho_ref):
    xn = x_ref[...] + _rms(y, gpost_ref[...])
    xo_ref[...] = xn
    if ho_ref is not None:
        ho_ref[...] = _rms(xn, gnext_ref[...]).astype(BF16)


def _prenorm_body(x_ref, g_ref, h_ref):
    h_ref[...] = _rms(x_ref[...], g_ref[...]).astype(BF16)


def _prenorm(x, g):
    t, d = x.shape
    tm = ROW_TILE
    return pl.pallas_call(
        _prenorm_body,
        grid=(t // tm,),
        in_specs=[pl.BlockSpec((tm, d), lambda i: (i, 0)), _resident(*g)],
        out_specs=pl.BlockSpec((tm, d), lambda i: (i, 0)),
        out_shape=jax.ShapeDtypeStruct((t, d), BF16),
        compiler_params=_params(),
        name="prenorm",
    )(x, g[0])


def _kv_body(mem_ref, g_ref, w_ref, kv_ref, memn_ref):
    @pl.when(pl.program_id(1) == 0)
    def _():
        memn_ref[...] = _rms(mem_ref[...], g_ref[...]).astype(BF16)

    kv_ref[...] = _dot(memn_ref[...], w_ref[...].astype(BF16)).astype(BF16)


def _memory_kv(mem2d, xa_norm4, wkv):
    rows, d = mem2d.shape
    depth, _, n = wkv.shape
    tn = PROJ_COLS
    return pl.pallas_call(
        _kv_body,
        grid=(depth, n // tn),
        in_specs=[
            pl.BlockSpec((rows, d), lambda l, j: (0, 0), pipeline_mode=pl.Buffered(1)),
            pl.BlockSpec((None, None, 1, d), lambda l, j: (l, 2, 0, 0)),
            pl.BlockSpec((None, d, tn), lambda l, j: (l, 0, j)),
        ],
        out_specs=pl.BlockSpec((None, rows, tn), lambda l, j: (l, 0, j)),
        out_shape=jax.ShapeDtypeStruct((depth, rows, n), BF16),
        scratch_shapes=[pltpu.VMEM((rows, d), BF16)],
        compiler_params=_params(),
        name="memory_kv",
    )(mem2d, xa_norm4, wkv)


def _proj_body(h_ref, *rest, n_parts, n_outs, combine):
    w_refs, out_refs = rest[:n_parts], rest[n_parts:n_parts + n_outs]
    wbf_ref = rest[n_parts + n_outs]

    @pl.when(pl.program_id(1) == 0)
    def _():
        for p in range(n_parts):
            wbf_ref[p] = w_refs[p][...].astype(BF16)

    h = h_ref[...]
    outs = combine(*[_dot(h, wbf_ref[p]) for p in range(n_parts)])
    for o_ref, o in zip(out_refs, outs):
        o_ref[...] = o.astype(BF16)


def _proj_call(combine, h, w, slot, n_parts, n_out_cols, n_outs, name):
    t, d = h.shape
    tm, tn = min(PROJ_ROWS, t), PROJ_COLS
    nj = n_out_cols // tn
    w_specs = [
        pl.BlockSpec((None, d, tn), functools.partial(lambda j, i, p: (slot, 0, p * nj + j), p=p))
        for p in range(n_parts)
    ]
    out_spec = pl.BlockSpec((tm, tn), lambda j, i: (i, j))
    out_shape = jax.ShapeDtypeStruct((t, n_out_cols), BF16)
    return pl.pallas_call(
        functools.partial(_proj_body, n_parts=n_parts, n_outs=n_outs, combine=combine),
        grid=(nj, t // tm),
        in_specs=[pl.BlockSpec((tm, d), lambda j, i: (i, 0))] + w_specs,
        out_specs=[out_spec] * n_outs if n_outs > 1 else out_spec,
        out_shape=[out_shape] * n_outs if n_outs > 1 else out_shape,
        scratch_shapes=[pltpu.VMEM((n_parts, d, tn), BF16)],
        compiler_params=_params(),
        name=name,
    )(h, *([w] * n_parts))


def _short_combine(b, c, z):
    return b, c * z


def _gmlp_combine(uv):
    return (_gelu_tanh(uv),)


def _conf_combine(a, g):
    return (a * _sigmoid(g),)


def _swiglu_combine(g, u):
    return ((g * _sigmoid(g)) * u,)


def _row_call(lhs_fn, row_inputs, halo, const_specs, const_args, x, g_post, g_next, scratch, name, seq):
    t, d = x.shape
    tm = ROW_TILE
    in_specs, args = [], []
    for arr, ncols, cb in row_inputs:
        in_specs.append(pl.BlockSpec((tm, ncols), functools.partial(lambda i, cb: (i, cb), cb=cb)))
        args.append(arr)
    if halo is not None:
        arr, hr = halo
        per_tile = tm // hr
        in_specs.append(pl.BlockSpec((hr, arr.shape[1]), lambda i: (jnp.maximum(i * per_tile - 1, 0), 0)))
        args.append(arr)
    in_specs += const_specs
    args += const_args
    row_spec = pl.BlockSpec((tm, d), lambda i: (i, 0))
    in_specs += [row_spec, _resident(*g_post)]
    args += [x, g_post[0]]
    has_next = g_next is not None
    if has_next:
        in_specs.append(_resident(*g_next))
        args.append(g_next[0])
        out_specs = [row_spec, row_spec]
        out_shape = [jax.ShapeDtypeStruct((t, d), F32), jax.ShapeDtypeStruct((t, d), BF16)]
    else:
        out_specs = row_spec
        out_shape = jax.ShapeDtypeStruct((t, d), F32)
    n_lead = len(row_inputs) + (halo is not None) + len(const_args)
    body = functools.partial(_row_body, lhs_fn=lhs_fn, n_lead=n_lead, has_next=has_next, seq_tiles=seq // tm)
    return pl.pallas_call(
        body,
        grid=(t // tm,),
        in_specs=in_specs,
        out_specs=out_specs,
        out_shape=out_shape,
        scratch_shapes=scratch,
        compiler_params=_params(),
        name=name,
    )(*args)


def _row_body(*refs, lhs_fn, n_lead, has_next, seq_tiles):
    lead, rest = refs[:n_lead], refs[n_lead:]
    if has_next:
        x_ref, gpost_ref, gnext_ref, xo_ref, ho_ref = rest[:5]
        scratch = rest[5:]
    else:
        x_ref, gpost_ref, xo_ref = rest[:3]
        gnext_ref = ho_ref = None
        scratch = rest[3:]
    at_seq_start = lax.rem(pl.program_id(0), seq_tiles) == 0
    y = lhs_fn(*lead, *scratch, at_seq_start=at_seq_start)
    _finish(y, x_ref, gpost_ref, gnext_ref, xo_ref, ho_ref)


def _short_lhs(b_ref, cz_ref, halo_ref, cw_ref, w_ref, ext_ref, *, at_seq_start):
    tm = cz_ref.shape[0]
    ext_ref[0:SHORT_HALO, :] = jnp.where(at_seq_start, 0.0, halo_ref[...].astype(F32))
    ext_ref[SHORT_HALO:, :] = cz_ref[...].astype(F32)
    cw = cw_ref[...]
    y = cw[2:3, :] * ext_ref[pl.ds(SHORT_HALO, tm), :]
    y = y + cw[1:2, :] * ext_ref[pl.ds(SHORT_HALO - 1, tm), :]
    y = y + cw[0:1, :] * ext_ref[pl.ds(SHORT_HALO - 2, tm), :]
    return _dot((b_ref[...].astype(F32) * y).astype(BF16), w_ref[...])


def _gmlp_lhs(u_ref, v_ref, lng_ref, lnb_ref, ws_ref, sb_ref, w_ref, lhs_ref, *, at_seq_start):
    del at_seq_start
    tm, hidden = u_ref.shape
    gdim = hidden // GMLP_GROUPS
    vn = _layernorm(v_ref[...].astype(F32), lng_ref[...], lnb_ref[...]).astype(BF16)
    row = lax.broadcasted_iota(jnp.int32, (CHUNK, CHUNK), 0)
    col = lax.broadcasted_iota(jnp.int32, (CHUNK, CHUNK), 1)
    causal = col <= row
    sb = sb_ref[...]
    for g in range(GMLP_GROUPS):
        ws = jnp.where(causal, ws_ref[g], 0.0).astype(BF16)
        bias = sb[:, g:g + 1]
        cols = slice(g * gdim, (g + 1) * gdim)
        for c in range(tm // CHUNK):
            rows = slice(c * CHUNK, (c + 1) * CHUNK)
            sv = _dot(ws, vn[rows, cols]) + bias
            lhs_ref[rows, cols] = (u_ref[rows, cols].astype(F32) * sv).astype(BF16)
    return _dot(lhs_ref[...], w_ref[...])


def _conf_lhs(y_ref, halo_ref, cw_ref, cb_ref, lng_ref, lnb_ref, w_ref, ext_ref, acc_ref, *, at_seq_start):
    tm, d = y_ref.shape
    ext_ref[0:CONF_HALO, :] = jnp.where(at_seq_start, 0.0, halo_ref[...].astype(F32))
    ext_ref[CONF_HALO:, :] = y_ref[...].astype(F32)
    rchunk, cchunk = 128, 128
    span = rchunk + SUBLANES
    for cc in range(d // cchunk):
        cols = slice(cc * cchunk, (cc + 1) * cchunk)
        for rc in range(tm // rchunk):
            first = CONF_HALO + rc * rchunk - SUBLANES
            out = None
            for b in range(SUBLANES):
                p_b = None
                for a in range((CONF_CONV - 1 - b) // SUBLANES + 1):
                    lag = SUBLANES * a + b
                    k = CONF_CONV - 1 - lag
                    term = cw_ref[k:k + 1, cols] * ext_ref[pl.ds(first - SUBLANES * a, span), cols]
                    p_b = term if p_b is None else p_b + term
                shifted = p_b if b == 0 else pltpu.roll(p_b, b, 0)
                out = shifted if out is None else out + shifted
            out = out[SUBLANES:]
            acc_ref[rc * rchunk:(rc + 1) * rchunk, cols] = out
    z = _layernorm(acc_ref[...] + cb_ref[...], lng_ref[...], lnb_ref[...])
    return _dot((z * _sigmoid(z)).astype(BF16), w_ref[...])


def _xattn_lhs(h_ref, kv_ref, wq_ref, wo_ref, o_ref, *, at_seq_start):
    del at_seq_start
    d = h_ref.shape[1]
    hd = d // XA_HEADS
    scale = hd ** -0.5
    q = _dot(h_ref[...], wq_ref[...]).astype(BF16)
    for hh in range(XA_HEADS):
        cols = slice(hh * hd, (hh + 1) * hd)
        k = kv_ref[:, hh * hd:(hh + 1) * hd]
        v = kv_ref[:, d + hh * hd:d + (hh + 1) * hd]
        s = lax.dot_general(q[:, cols], k, (((1,), (1,)), ((), ())), preferred_element_type=F32) * scale
        e = jnp.exp(s - jnp.max(s, axis=-1, keepdims=True))
        p = (e * (1.0 / jnp.sum(e, axis=-1, keepdims=True))).astype(BF16)
        o_ref[:, cols] = _dot(p, v).astype(BF16)
    return _dot(o_ref[...], wo_ref[...])


def _down_lhs(a_ref, w_ref, *, at_seq_start):
    del at_seq_start
    return _dot(a_ref[...], w_ref[...])


def kernel(x, mem, mix_norm, xa_norm, xa_wq, xa_wkv, xa_wo, ffn_norm, ffn_w_gu, ffn_w_down,
           a_w_in, a_conv_w, a_w_out, b_w_in, b_v_g, b_v_b, b_w_s, b_s_bias, b_w_out,
           c_w_in, c_conv_w, c_conv_b, c_ln_g, c_ln_b, c_w_out):
    bsz, seq, d = x.shape
    t = bsz * seq
    assert seq % ROW_TILE == 0 and ROW_TILE % CHUNK == 0
    d_ff = ffn_w_down.shape[1]
    tiles_per_seq = seq // ROW_TILE

    mix_norm4, xa_norm4, ffn_norm4 = (v[:, :, None, :] for v in (mix_norm, xa_norm, ffn_norm))
    rows3 = lambda v: v[:, None, :]
    b_v_g3, b_v_b3, c_conv_b3, c_ln_g3, c_ln_b3 = map(rows3, (b_v_g, b_v_b, c_conv_b, c_ln_g, c_ln_b))
    b_s_bias_t = jnp.swapaxes(b_s_bias, 1, 2)
    wq, wo, w_down = xa_wq.astype(BF16), xa_wo.astype(BF16), ffn_w_down.astype(BF16)
    a_wo, b_wo, c_wo = a_w_out.astype(BF16), b_w_out.astype(BF16), c_w_out.astype(BF16)

    x2 = x.reshape(t, d)
    kv = _memory_kv(mem.reshape(bsz * N_MEM, d), xa_norm4, xa_wkv)
    h = _prenorm(x2, (mix_norm4, 0, 0))

    for i in range(DEPTH):
        kind, slot = i % N_MIXERS, i // N_MIXERS
        g_post, g_next = (mix_norm4, i, 1), (xa_norm4, i, 0)
        if kind == 0:
            b, cz = _proj_call(_short_combine, h, a_w_in, slot, 3, d, 2, "short_in")
            consts = [(a_conv_w, slot), (a_wo, slot)]
            x2, h = _row_call(
                _short_lhs, [(b, d, 0), (cz, d, 0)], (cz, SHORT_HALO),
                [_resident(*c) for c in consts], [c[0] for c in consts], x2, g_post, g_next,
                [pltpu.VMEM((ROW_TILE + SHORT_HALO, d), F32)], "short_out", seq)
        elif kind == 1:
            uv = _proj_call(_gmlp_combine, h, b_w_in, slot, 1, 2 * d, 1, "gmlp_in")
            consts = [(b_v_g3, slot), (b_v_b3, slot), (b_w_s, slot), (b_s_bias_t, slot), (b_wo, slot)]
            x2, h = _row_call(
                _gmlp_lhs, [(uv, d, 0), (uv, d, 1)], None,
                [_resident(*c) for c in consts], [c[0] for c in consts], x2, g_post, g_next,
                [pltpu.VMEM((ROW_TILE, d), BF16)], "gmlp_out", seq)
        else:
            y0 = _proj_call(_conf_combine, h, c_w_in, slot, 2, d, 1, "conf_in")
            consts = [(c_conv_w, slot), (c_conv_b3, slot), (c_ln_g3, slot), (c_ln_b3, slot), (c_wo, slot)]
            x2, h = _row_call(
                _conf_lhs, [(y0, d, 0)], (y0, CONF_HALO),
                [_resident(*c) for c in consts], [c[0] for c in consts], x2, g_post, g_next,
                [pltpu.VMEM((ROW_TILE + CONF_HALO, d), F32), pltpu.VMEM((ROW_TILE, d), F32)], "conf_out", seq)

        kv_spec = pl.BlockSpec((None, N_MEM, 2 * d), lambda r: (i, r // tiles_per_seq, 0))
        x2, h = _row_call(
            _xattn_lhs, [(h, d, 0)], None,
            [kv_spec, _resident(wq, i), _resident(wo, i)], [kv, wq, wo],
            x2, (xa_norm4, i, 1), (ffn_norm4, i, 0), [pltpu.VMEM((ROW_TILE, d), BF16)], "xattn", seq)

        a = _proj_call(_swiglu_combine, h, ffn_w_gu, i, 2, d_ff, 1, "swiglu_in")
        last = i == DEPTH - 1
        g_next = None if last else (mix_norm4, i + 1, 0)
        out = _row_call(
            _down_lhs, [(a, d_ff, 0)], None, [_resident(w_down, i)], [w_down],
            x2, (ffn_norm4, i, 1), g_next, [], "swiglu_down", seq)
        x2, h = (out, None) if last else out

    return x2.reshape(bsz, seq, d)
```

```python
import functools

import jax
import jax.numpy as jnp
from jax import lax
from jax.experimental import pallas as pl
from jax.experimental.pallas import tpu as pltpu

DEPTH = 4
N_MIXERS = 3
N_MEM = 256
SHORT_CONV = 3
CHUNK = 128
GMLP_GROUPS = 8
CONF_CONV = 31
XA_HEADS = 4
EPS = 1e-6

F32 = jnp.float32
BF16 = jnp.bfloat16

ROW_TILE = 256
PROJ_ROWS = 1024
PROJ_COLS = 512
VMEM_LIMIT_BYTES = 56 * 1024 * 1024
SHORT_HALO = 16
CONF_HALO = 32
SUBLANES = 8
LANES = 128


def _params():
    return pltpu.CompilerParams(vmem_limit_bytes=VMEM_LIMIT_BYTES)


def _resident(arr, *lead):
    shape = arr.shape[len(lead):]
    idx = tuple(lead) + (0,) * len(shape)
    return pl.BlockSpec((None,) * len(lead) + shape, lambda *_: idx, pipeline_mode=pl.Buffered(1))


def _rms(v, g):
    ms = jnp.mean(v * v, axis=-1, keepdims=True)
    return (v * lax.rsqrt(ms + EPS)) * g


def _layernorm(v, g, b):
    mu = jnp.mean(v, axis=-1, keepdims=True)
    d = v - mu
    var = jnp.mean(d * d, axis=-1, keepdims=True)
    return (d * lax.rsqrt(var + EPS)) * g + b


def _sigmoid(v):
    return 0.5 * jnp.tanh(0.5 * v) + 0.5


def _gelu_tanh(v):
    c = 0.7978845608028654
    return v * (0.5 * (1.0 + jnp.tanh(c * (v + 0.044715 * (v * v * v)))))


def _dot(a, b):
    return jnp.dot(a, b, preferred_element_type=F32)


def _finish(y, x_ref, gpost_ref, gnext_ref, xo_ref, ho_ref):
    xn = x_ref[...] + _rms(y, gpost_ref[...])
    xo_ref[...] = xn
    if ho_ref is not None:
        ho_ref[...] = _rms(xn, gnext_ref[...]).astype(BF16)


def _prenorm_body(x_ref, g_ref, h_ref):
    h_ref[...] = _rms(x_ref[...], g_ref[...]).astype(BF16)


def _prenorm(x, g):
    t, d = x.shape
    tm = ROW_TILE
    return pl.pallas_call(
        _prenorm_body,
        grid=(t // tm,),
        in_specs=[pl.BlockSpec((tm, d), lambda i: (i, 0)), _resident(*g)],
        out_specs=pl.BlockSpec((tm, d), lambda i: (i, 0)),
        out_shape=jax.ShapeDtypeStruct((t, d), BF16),
        compiler_params=_params(),
        name="prenorm",
    )(x, g[0])


def _kv_body(mem_ref, g_ref, w_ref, kv_ref, memn_ref):
    @pl.when(pl.program_id(1) == 0)
    def _():
        memn_ref[...] = _rms(mem_ref[...], g_ref[...]).astype(BF16)

    kv_ref[...] = _dot(memn_ref[...], w_ref[...].astype(BF16)).astype(BF16)


def _memory_kv(mem2d, xa_norm4, wkv):
    rows, d = mem2d.shape
    depth, _, n = wkv.shape
    tn = PROJ_COLS
    return pl.pallas_call(
        _kv_body,
        grid=(depth, n // tn),
        in_specs=[
            pl.BlockSpec((rows, d), lambda l, j: (0, 0), pipeline_mode=pl.Buffered(1)),
            pl.BlockSpec((None, None, 1, d), lambda l, j: (l, 2, 0, 0)),
            pl.BlockSpec((None, d, tn), lambda l, j: (l, 0, j)),
        ],
        out_specs=pl.BlockSpec((None, rows, tn), lambda l, j: (l, 0, j)),
        out_shape=jax.ShapeDtypeStruct((depth, rows, n), BF16),
        scratch_shapes=[pltpu.VMEM((rows, d), BF16)],
        compiler_params=_params(),
        name="memory_kv",
    )(mem2d, xa_norm4, wkv)


def _proj_body(h_ref, *rest, n_parts, n_outs, combine):
    w_refs, out_refs = rest[:n_parts], rest[n_parts:n_parts + n_outs]
    wbf_ref = rest[n_parts + n_outs]
    groups = w_refs[0].shape[1] // LANES

    def col(q, p):
        return slice((q * n_parts + p) * LANES, (q * n_parts + p + 1) * LANES)

    @pl.when(pl.program_id(1) == 0)
    def _():
        for p in range(n_parts):
            for q in range(groups):
                wbf_ref[:, col(q, p)] = w_refs[p][:, q * LANES:(q + 1) * LANES].astype(BF16)

    r = _dot(h_ref[...], wbf_ref[...])
    for q in range(groups):
        outs = combine(*[r[:, col(q, p)] for p in range(n_parts)])
        for o_ref, o in zip(out_refs, outs):
            o_ref[:, q * LANES:(q + 1) * LANES] = o.astype(BF16)


def _proj_call(combine, h, w, slot, n_parts, n_out_cols, n_outs, name, tn=PROJ_COLS):
    t, d = h.shape
    tm = min(PROJ_ROWS, t)
    nj = n_out_cols // tn
    w_specs = [
        pl.BlockSpec((None, d, tn), functools.partial(lambda j, i, p: (slot, 0, p * nj + j), p=p))
        for p in range(n_parts)
    ]
    out_spec = pl.BlockSpec((tm, tn), lambda j, i: (i, j))
    out_shape = jax.ShapeDtypeStruct((t, n_out_cols), BF16)
    return pl.pallas_call(
        functools.partial(_proj_body, n_parts=n_parts, n_outs=n_outs, combine=combine),
        grid=(nj, t // tm),
        in_specs=[pl.BlockSpec((tm, d), lambda j, i: (i, 0))] + w_specs,
        out_specs=[out_spec] * n_outs if n_outs > 1 else out_spec,
        out_shape=[out_shape] * n_outs if n_outs > 1 else out_shape,
        scratch_shapes=[pltpu.VMEM((d, n_parts * tn), BF16)],
        compiler_params=_params(),
        name=name,
    )(h, *([w] * n_parts))


def _short_combine(b, c, z):
    return b, c * z


def _gmlp_combine(uv):
    return (_gelu_tanh(uv),)


def _conf_combine(a, g):
    return (a * _sigmoid(g),)


def _swiglu_combine(g, u):
    return ((g * _sigmoid(g)) * u,)


def _row_call(lhs_fn, row_inputs, halo, const_specs, const_args, x, g_post, g_next, scratch, name, seq):
    t, d = x.shape
    tm = ROW_TILE
    in_specs, args = [], []
    for arr, ncols, cb in row_inputs:
        in_specs.append(pl.BlockSpec((tm, ncols), functools.partial(lambda i, cb: (i, cb), cb=cb)))
        args.append(arr)
    if halo is not None:
        arr, hr = halo
        per_tile = tm // hr
        in_specs.append(pl.BlockSpec((hr, arr.shape[1]), lambda i: (jnp.maximum(i * per_tile - 1, 0), 0)))
        args.append(arr)
    in_specs += const_specs
    args += const_args
    row_spec = pl.BlockSpec((tm, d), lambda i: (i, 0))
    in_specs += [row_spec, _resident(*g_post)]
    args += [x, g_post[0]]
    has_next = g_next is not None
    if has_next:
        in_specs.append(_resident(*g_next))
        args.append(g_next[0])
        out_specs = [row_spec, row_spec]
        out_shape = [jax.ShapeDtypeStruct((t, d), F32), jax.ShapeDtypeStruct((t, d), BF16)]
    else:
        out_specs = row_spec
        out_shape = jax.ShapeDtypeStruct((t, d), F32)
    n_lead = len(row_inputs) + (halo is not None) + len(const_args)
    body = functools.partial(_row_body, lhs_fn=lhs_fn, n_lead=n_lead, has_next=has_next, seq_tiles=seq // tm)
    return pl.pallas_call(
        body,
        grid=(t // tm,),
        in_specs=in_specs,
        out_specs=out_specs,
        out_shape=out_shape,
        scratch_shapes=scratch,
        compiler_params=_params(),
        name=name,
    )(*args)


def _row_body(*refs, lhs_fn, n_lead, has_next, seq_tiles):
    lead, rest = refs[:n_lead], refs[n_lead:]
    if has_next:
        x_ref, gpost_ref, gnext_ref, xo_ref, ho_ref = rest[:5]
        scratch = rest[5:]
    else:
        x_ref, gpost_ref, xo_ref = rest[:3]
        gnext_ref = ho_ref = None
        scratch = rest[3:]
    at_seq_start = lax.rem(pl.program_id(0), seq_tiles) == 0
    y = lhs_fn(*lead, *scratch, at_seq_start=at_seq_start)
    _finish(y, x_ref, gpost_ref, gnext_ref, xo_ref, ho_ref)


def _short_lhs(b_ref, cz_ref, halo_ref, cw_ref, w_ref, ext_ref, *, at_seq_start):
    tm = cz_ref.shape[0]
    ext_ref[0:SHORT_HALO, :] = jnp.where(at_seq_start, 0.0, halo_ref[...].astype(F32))
    ext_ref[SHORT_HALO:, :] = cz_ref[...].astype(F32)
    cw = cw_ref[...]
    y = cw[2:3, :] * ext_ref[pl.ds(SHORT_HALO, tm), :]
    y = y + cw[1:2, :] * ext_ref[pl.ds(SHORT_HALO - 1, tm), :]
    y = y + cw[0:1, :] * ext_ref[pl.ds(SHORT_HALO - 2, tm), :]
    return _dot((b_ref[...].astype(F32) * y).astype(BF16), w_ref[...])


def _gmlp_lhs(u_ref, v_ref, lng_ref, lnb_ref, ws_ref, sb_ref, w_ref, lhs_ref, *, at_seq_start):
    del at_seq_start
    tm, hidden = u_ref.shape
    gdim = hidden // GMLP_GROUPS
    vn = _layernorm(v_ref[...].astype(F32), lng_ref[...], lnb_ref[...]).astype(BF16)
    row = lax.broadcasted_iota(jnp.int32, (CHUNK, CHUNK), 0)
    col = lax.broadcasted_iota(jnp.int32, (CHUNK, CHUNK), 1)
    causal = col <= row
    sb = sb_ref[...]
    for g in range(GMLP_GROUPS):
        ws = jnp.where(causal, ws_ref[g], 0.0).astype(BF16)
        bias = sb[:, g:g + 1]
        cols = slice(g * gdim, (g + 1) * gdim)
        for c in range(tm // CHUNK):
            rows = slice(c * CHUNK, (c + 1) * CHUNK)
            sv = _dot(ws, vn[rows, cols]) + bias
            lhs_ref[rows, cols] = (u_ref[rows, cols].astype(F32) * sv).astype(BF16)
    return _dot(lhs_ref[...], w_ref[...])


def _conf_lhs(y_ref, halo_ref, cw_ref, cb_ref, lng_ref, lnb_ref, w_ref, ext_ref, acc_ref, *, at_seq_start):
    tm, d = y_ref.shape
    ext_ref[0:CONF_HALO, :] = jnp.where(at_seq_start, 0.0, halo_ref[...].astype(F32))
    ext_ref[CONF_HALO:, :] = y_ref[...].astype(F32)
    rchunk, cchunk = 128, 128
    span = rchunk + SUBLANES
    for cc in range(d // cchunk):
        cols = slice(cc * cchunk, (cc + 1) * cchunk)
        for rc in range(tm // rchunk):
            first = CONF_HALO + rc * rchunk - SUBLANES
            out = None
            for b in range(SUBLANES):
                p_b = None
                for a in range((CONF_CONV - 1 - b) // SUBLANES + 1):
                    lag = SUBLANES * a + b
                    k = CONF_CONV - 1 - lag
                    term = cw_ref[k:k + 1, cols] * ext_ref[pl.ds(first - SUBLANES * a, span), cols]
                    p_b = term if p_b is None else p_b + term
                shifted = p_b if b == 0 else pltpu.roll(p_b, b, 0)
                out = shifted if out is None else out + shifted
            out = out[SUBLANES:]
            acc_ref[rc * rchunk:(rc + 1) * rchunk, cols] = out
    z = _layernorm(acc_ref[...] + cb_ref[...], lng_ref[...], lnb_ref[...])
    return _dot((z * _sigmoid(z)).astype(BF16), w_ref[...])


def _xattn_lhs(h_ref, kv_ref, wq_ref, wo_ref, o_ref, *, at_seq_start):
    del at_seq_start
    d = h_ref.shape[1]
    hd = d // XA_HEADS
    scale = hd ** -0.5
    q = _dot(h_ref[...], wq_ref[...]).astype(BF16)
    for hh in range(XA_HEADS):
        cols = slice(hh * hd, (hh + 1) * hd)
        k = kv_ref[:, hh * hd:(hh + 1) * hd]
        v = kv_ref[:, d + hh * hd:d + (hh + 1) * hd]
        s = lax.dot_general(q[:, cols], k, (((1,), (1,)), ((), ())), preferred_element_type=F32) * scale
        e = jnp.exp(s - jnp.max(s, axis=-1, keepdims=True))
        p = (e * (1.0 / jnp.sum(e, axis=-1, keepdims=True))).astype(BF16)
        o_ref[:, cols] = _dot(p, v).astype(BF16)
    return _dot(o_ref[...], wo_ref[...])


def _down_lhs(a_ref, w_ref, *, at_seq_start):
    del at_seq_start
    return _dot(a_ref[...], w_ref[...])


def kernel(x, mem, mix_norm, xa_norm, xa_wq, xa_wkv, xa_wo, ffn_norm, ffn_w_gu, ffn_w_down,
           a_w_in, a_conv_w, a_w_out, b_w_in, b_v_g, b_v_b, b_w_s, b_s_bias, b_w_out,
           c_w_in, c_conv_w, c_conv_b, c_ln_g, c_ln_b, c_w_out):
    bsz, seq, d = x.shape
    t = bsz * seq
    assert seq % ROW_TILE == 0 and ROW_TILE % CHUNK == 0
    d_ff = ffn_w_down.shape[1]
    tiles_per_seq = seq // ROW_TILE

    mix_norm4, xa_norm4, ffn_norm4 = (v[:, :, None, :] for v in (mix_norm, xa_norm, ffn_norm))
    rows3 = lambda v: v[:, None, :]
    b_v_g3, b_v_b3, c_conv_b3, c_ln_g3, c_ln_b3 = map(rows3, (b_v_g, b_v_b, c_conv_b, c_ln_g, c_ln_b))
    b_s_bias_t = jnp.swapaxes(b_s_bias, 1, 2)
    wq, wo, w_down = xa_wq.astype(BF16), xa_wo.astype(BF16), ffn_w_down.astype(BF16)
    a_wo, b_wo, c_wo = a_w_out.astype(BF16), b_w_out.astype(BF16), c_w_out.astype(BF16)

    x2 = x.reshape(t, d)
    kv = _memory_kv(mem.reshape(bsz * N_MEM, d), xa_norm4, xa_wkv)
    h = _prenorm(x2, (mix_norm4, 0, 0))

    for i in range(DEPTH):
        kind, slot = i % N_MIXERS, i // N_MIXERS
        g_post, g_next = (mix_norm4, i, 1), (xa_norm4, i, 0)
        if kind == 0:
            b, cz = _proj_call(_short_combine, h, a_w_in, slot, 3, d, 2, "short_in")
            consts = [(a_conv_w, slot), (a_wo, slot)]
            x2, h = _row_call(
                _short_lhs, [(b, d, 0), (cz, d, 0)], (cz, SHORT_HALO),
                [_resident(*c) for c in consts], [c[0] for c in consts], x2, g_post, g_next,
                [pltpu.VMEM((ROW_TILE + SHORT_HALO, d), F32)], "short_out", seq)
        elif kind == 1:
            uv = _proj_call(_gmlp_combine, h, b_w_in, slot, 1, 2 * d, 1, "gmlp_in", tn=2 * PROJ_COLS)
            consts = [(b_v_g3, slot), (b_v_b3, slot), (b_w_s, slot), (b_s_bias_t, slot), (b_wo, slot)]
            x2, h = _row_call(
                _gmlp_lhs, [(uv, d, 0), (uv, d, 1)], None,
                [_resident(*c) for c in consts], [c[0] for c in consts], x2, g_post, g_next,
                [pltpu.VMEM((ROW_TILE, d), BF16)], "gmlp_out", seq)
        else:
            y0 = _proj_call(_conf_combine, h, c_w_in, slot, 2, d, 1, "conf_in")
            consts = [(c_conv_w, slot), (c_conv_b3, slot), (c_ln_g3, slot), (c_ln_b3, slot), (c_wo, slot)]
            x2, h = _row_call(
                _conf_lhs, [(y0, d, 0)], (y0, CONF_HALO),
                [_resident(*c) for c in consts], [c[0] for c in consts], x2, g_post, g_next,
                [pltpu.VMEM((ROW_TILE + CONF_HALO, d), F32), pltpu.VMEM((ROW_TILE, d), F32)], "conf_out", seq)

        kv_spec = pl.BlockSpec((None, N_MEM, 2 * d), lambda r: (i, r // tiles_per_seq, 0))
        x2, h = _row_call(
            _xattn_lhs, [(h, d, 0)], None,
            [kv_spec, _resident(wq, i), _resident(wo, i)], [kv, wq, wo],
            x2, (xa_norm4, i, 1), (ffn_norm4, i, 0), [pltpu.VMEM((ROW_TILE, d), BF16)], "xattn", seq)

        a = _proj_call(_swiglu_combine, h, ffn_w_gu, i, 2, d_ff, 1, "swiglu_in")
        last = i == DEPTH - 1
        g_next = None if last else (mix_norm4, i + 1, 0)
        out = _row_call(
            _down_lhs, [(a, d_ff, 0)], None, [_resident(w_down, i)], [w_down],
            x2, (ffn_norm4, i, 1), g_next, [], "swiglu_down", seq)
        x2, h = (out, None) if last else out

    return x2.reshape(bsz, seq, d)
```

```python
import functools

import jax
import jax.numpy as jnp
from jax import lax
from jax.experimental import pallas as pl
from jax.experimental.pallas import tpu as pltpu

DEPTH = 4
N_MIXERS = 3
N_MEM = 256
SHORT_CONV = 3
CHUNK = 128
GMLP_GROUPS = 8
CONF_CONV = 31
XA_HEADS = 4
EPS = 1e-6

F32 = jnp.float32
BF16 = jnp.bfloat16

ROW_TILE = 256
PROJ_ROWS = 1024
PROJ_COLS = 512
VMEM_LIMIT_BYTES = 56 * 1024 * 1024
SHORT_HALO = 16
CONF_HALO = 32
SUBLANES = 8
LANES = 128


def _params():
    return pltpu.CompilerParams(vmem_limit_bytes=VMEM_LIMIT_BYTES)


def _resident(arr, *lead):
    shape = arr.shape[len(lead):]
    idx = tuple(lead) + (0,) * len(shape)
    return pl.BlockSpec((None,) * len(lead) + shape, lambda *_: idx, pipeline_mode=pl.Buffered(1))


def _rms(v, g):
    ms = jnp.mean(v * v, axis=-1, keepdims=True)
    return (v * lax.rsqrt(ms + EPS)) * g


def _layernorm(v, g, b):
    mu = jnp.mean(v, axis=-1, keepdims=True)
    d = v - mu
    var = jnp.mean(d * d, axis=-1, keepdims=True)
    return (d * lax.rsqrt(var + EPS)) * g + b


def _sigmoid(v):
    return 0.5 * jnp.tanh(0.5 * v) + 0.5


def _gelu_tanh(v):
    c = 0.7978845608028654
    return v * (0.5 * (1.0 + jnp.tanh(c * (v + 0.044715 * (v * v * v)))))


def _dot(a, b):
    return jnp.dot(a, b, preferred_element_type=F32)


def _finish(y, x_ref, gpost_ref, gnext_ref, xo_ref, ho_ref):
    xn = x_ref[...] + _rms(y, gpost_ref[...])
    xo_ref[...] = xn
    if ho_ref is not None:
        ho_ref[...] = _rms(xn, gnext_ref[...]).astype(BF16)


def _prenorm_body(x_ref, g_ref, h_ref):
    h_ref[...] = _rms(x_ref[...], g_ref[...]).astype(BF16)


def _prenorm(x, g):
    t, d = x.shape
    tm = min(PROJ_ROWS, t)
    return pl.pallas_call(
        _prenorm_body,
        grid=(t // tm,),
        in_specs=[pl.BlockSpec((tm, d), lambda i: (i, 0)), _resident(*g)],
        out_specs=pl.BlockSpec((tm, d), lambda i: (i, 0)),
        out_shape=jax.ShapeDtypeStruct((t, d), BF16),
        compiler_params=_params(),
        name="prenorm",
    )(x, g[0])


def _kv_body(mem_ref, g_ref, w_ref, kv_ref, memn_ref):
    @pl.when(pl.program_id(1) == 0)
    def _():
        memn_ref[...] = _rms(mem_ref[...], g_ref[...]).astype(BF16)

    kv_ref[...] = _dot(memn_ref[...], w_ref[...].astype(BF16)).astype(BF16)


def _memory_kv(mem2d, xa_norm4, wkv):
    rows, d = mem2d.shape
    depth, _, n = wkv.shape
    tn = PROJ_COLS
    return pl.pallas_call(
        _kv_body,
        grid=(depth, n // tn),
        in_specs=[
            pl.BlockSpec((rows, d), lambda l, j: (0, 0), pipeline_mode=pl.Buffered(1)),
            pl.BlockSpec((None, None, 1, d), lambda l, j: (l, 2, 0, 0)),
            pl.BlockSpec((None, d, tn), lambda l, j: (l, 0, j)),
        ],
        out_specs=pl.BlockSpec((None, rows, tn), lambda l, j: (l, 0, j)),
        out_shape=jax.ShapeDtypeStruct((depth, rows, n), BF16),
        scratch_shapes=[pltpu.VMEM((rows, d), BF16)],
        compiler_params=_params(),
        name="memory_kv",
    )(mem2d, xa_norm4, wkv)


def _fold_qk_body(wq_ref, k_ref, qk_ref):
    r = lax.dot_general(wq_ref[...].astype(BF16), k_ref[...], (((1,), (1,)), ((), ())),
                        preferred_element_type=F32)
    for b in range(qk_ref.shape[0]):
        qk_ref[b] = r[:, b * N_MEM:(b + 1) * N_MEM].astype(BF16)


def _fold_vo_body(v_ref, wo_ref, vo_ref):
    r = _dot(v_ref[...], wo_ref[...].astype(BF16))
    for b in range(vo_ref.shape[0]):
        vo_ref[b] = r[b * N_MEM:(b + 1) * N_MEM, :].astype(BF16)


def _fold_memory(kv, wq, wo, bsz):
    depth, d, _ = wq.shape
    hd = d // XA_HEADS
    nb = min(4, bsz)
    grid = (depth, XA_HEADS, bsz // nb)
    qk = pl.pallas_call(
        _fold_qk_body,
        grid=grid,
        in_specs=[
            pl.BlockSpec((None, d, hd), lambda l, h, g: (l, 0, h)),
            pl.BlockSpec((None, nb * N_MEM, hd), lambda l, h, g: (l, g, h)),
        ],
        out_specs=pl.BlockSpec((None, nb, d, N_MEM), lambda l, h, g: (l, g, 0, h)),
        out_shape=jax.ShapeDtypeStruct((depth, bsz, d, XA_HEADS * N_MEM), BF16),
        compiler_params=_params(),
        name="fold_qk",
    )(wq, kv)
    vo = pl.pallas_call(
        _fold_vo_body,
        grid=grid,
        in_specs=[
            pl.BlockSpec((None, nb * N_MEM, hd), lambda l, h, g: (l, g, XA_HEADS + h)),
            pl.BlockSpec((None, hd, d), lambda l, h, g: (l, h, 0)),
        ],
        out_specs=pl.BlockSpec((None, nb, N_MEM, d), lambda l, h, g: (l, g, h, 0)),
        out_shape=jax.ShapeDtypeStruct((depth, bsz, XA_HEADS * N_MEM, d), BF16),
        compiler_params=_params(),
        name="fold_vo",
    )(kv, wo)
    return qk, vo


def _cast_weights(w_refs, wbf_ref):
    @pl.when(pl.program_id(1) == 0)
    def _():
        for p, w_ref in enumerate(w_refs):
            wbf_ref[p] = w_ref[...].astype(BF16)


def _proj_body(h_ref, *rest, n_parts, n_outs, combine):
    w_refs, out_refs = rest[:n_parts], rest[n_parts:n_parts + n_outs]
    wbf_ref = rest[n_parts + n_outs]
    _cast_weights(w_refs, wbf_ref)
    h = h_ref[...]
    outs = combine(*[_dot(h, wbf_ref[p]) for p in range(n_parts)])
    for o_ref, o in zip(out_refs, outs):
        o_ref[...] = o.astype(BF16)


def _conf_in_body(h_ref, wa_ref, wg_ref, cw_ref, cb_ref, o_ref, wbf_ref, ext_ref, *, seq_tiles):
    _cast_weights((wa_ref, wg_ref), wbf_ref)
    tm = o_ref.shape[0]
    at_seq_start = lax.rem(pl.program_id(1), seq_tiles) == 0

    @pl.when(at_seq_start)
    def _():
        ext_ref[0:CONF_HALO, :] = jnp.zeros((CONF_HALO, ext_ref.shape[1]), F32)

    @pl.when(jnp.logical_not(at_seq_start))
    def _():
        ext_ref[0:CONF_HALO, :] = ext_ref[tm:tm + CONF_HALO, :]

    h = h_ref[...]
    ext_ref[CONF_HALO:, :] = _dot(h, wbf_ref[0]) * _sigmoid(_dot(h, wbf_ref[1]))
    _causal_conv31(ext_ref, cw_ref, cb_ref, o_ref)


def _causal_conv31(ext_ref, cw_ref, cb_ref, o_ref):
    tm, tn = o_ref.shape
    rchunk, cchunk = 128, LANES
    span = rchunk + SUBLANES
    for cc in range(tn // cchunk):
        cols = slice(cc * cchunk, (cc + 1) * cchunk)
        for rc in range(tm // rchunk):
            first = CONF_HALO + rc * rchunk - SUBLANES
            out = None
            for b in range(SUBLANES):
                p_b = None
                for a in range((CONF_CONV - 1 - b) // SUBLANES + 1):
                    k = CONF_CONV - 1 - (SUBLANES * a + b)
                    term = cw_ref[k:k + 1, cols] * ext_ref[pl.ds(first - SUBLANES * a, span), cols]
                    p_b = term if p_b is None else p_b + term
                rolled = p_b if b == 0 else pltpu.roll(p_b, b, 0)
                out = rolled if out is None else out + rolled
            o_ref[rc * rchunk:(rc + 1) * rchunk, cols] = out[SUBLANES:] + cb_ref[:, cols]


def _proj_call(combine, h, w, slot, n_parts, n_out_cols, n_outs, name, tn=PROJ_COLS):
    t, d = h.shape
    tm = min(PROJ_ROWS, t)
    nj = n_out_cols // tn
    w_specs = [
        pl.BlockSpec((None, d, tn), functools.partial(lambda j, i, p: (slot, 0, p * nj + j), p=p))
        for p in range(n_parts)
    ]
    out_spec = pl.BlockSpec((tm, tn), lambda j, i: (i, j))
    out_shape = jax.ShapeDtypeStruct((t, n_out_cols), BF16)
    return pl.pallas_call(
        functools.partial(_proj_body, n_parts=n_parts, n_outs=n_outs, combine=combine),
        grid=(nj, t // tm),
        in_specs=[pl.BlockSpec((tm, d), lambda j, i: (i, 0))] + w_specs,
        out_specs=[out_spec] * n_outs if n_outs > 1 else out_spec,
        out_shape=[out_shape] * n_outs if n_outs > 1 else out_shape,
        scratch_shapes=[pltpu.VMEM((n_parts, d, tn), BF16)],
        compiler_params=_params(),
        name=name,
    )(h, *([w] * n_parts))


def _conf_in_call(h, w, conv_w, conv_b3, slot, seq):
    t, d = h.shape
    tm, tn = min(PROJ_ROWS, t), PROJ_COLS
    nj = d // tn
    w_specs = [
        pl.BlockSpec((None, d, tn), functools.partial(lambda j, i, p: (slot, 0, p * nj + j), p=p))
        for p in range(2)
    ]
    return pl.pallas_call(
        functools.partial(_conf_in_body, seq_tiles=seq // tm),
        grid=(nj, t // tm),
        in_specs=[pl.BlockSpec((tm, d), lambda j, i: (i, 0))] + w_specs + [
            pl.BlockSpec((None, CONF_CONV, tn), lambda j, i: (slot, 0, j)),
            pl.BlockSpec((None, 1, tn), lambda j, i: (slot, 0, j)),
        ],
        out_specs=pl.BlockSpec((tm, tn), lambda j, i: (i, j)),
        out_shape=jax.ShapeDtypeStruct((t, d), F32),
        scratch_shapes=[pltpu.VMEM((2, d, tn), BF16), pltpu.VMEM((tm + CONF_HALO, tn), F32)],
        compiler_params=_params(),
        name="conf_in",
    )(h, w, w, conv_w, conv_b3)


def _short_combine(b, c, z):
    return b, c * z


def _gmlp_combine(uv):
    return (_gelu_tanh(uv),)


def _swiglu_combine(g, u):
    return ((g * _sigmoid(g)) * u,)


def _row_call(lhs_fn, row_inputs, halo, const_specs, const_args, x, g_post, g_next, scratch, name, seq):
    t, d = x.shape
    tm = ROW_TILE
    in_specs, args = [], []
    for arr, ncols, cb in row_inputs:
        in_specs.append(pl.BlockSpec((tm, ncols), functools.partial(lambda i, cb: (i, cb), cb=cb)))
        args.append(arr)
    if halo is not None:
        arr, hr = halo
        per_tile = tm // hr
        in_specs.append(pl.BlockSpec((hr, arr.shape[1]), lambda i: (jnp.maximum(i * per_tile - 1, 0), 0)))
        args.append(arr)
    in_specs += const_specs
    args += const_args
    row_spec = pl.BlockSpec((tm, d), lambda i: (i, 0))
    in_specs += [row_spec, _resident(*g_post)]
    args += [x, g_post[0]]
    has_next = g_next is not None
    if has_next:
        in_specs.append(_resident(*g_next))
        args.append(g_next[0])
        out_specs = [row_spec, row_spec]
        out_shape = [jax.ShapeDtypeStruct((t, d), F32), jax.ShapeDtypeStruct((t, d), BF16)]
    else:
        out_specs = row_spec
        out_shape = jax.ShapeDtypeStruct((t, d), F32)
    n_lead = len(row_inputs) + (halo is not None) + len(const_args)
    body = functools.partial(_row_body, lhs_fn=lhs_fn, n_lead=n_lead, has_next=has_next, seq_tiles=seq // tm)
    return pl.pallas_call(
        body,
        grid=(t // tm,),
        in_specs=in_specs,
        out_specs=out_specs,
        out_shape=out_shape,
        scratch_shapes=scratch,
        compiler_params=_params(),
        name=name,
    )(*args)


def _row_body(*refs, lhs_fn, n_lead, has_next, seq_tiles):
    lead, rest = refs[:n_lead], refs[n_lead:]
    if has_next:
        x_ref, gpost_ref, gnext_ref, xo_ref, ho_ref = rest[:5]
        scratch = rest[5:]
    else:
        x_ref, gpost_ref, xo_ref = rest[:3]
        gnext_ref = ho_ref = None
        scratch = rest[3:]
    at_seq_start = lax.rem(pl.program_id(0), seq_tiles) == 0
    y = lhs_fn(*lead, *scratch, at_seq_start=at_seq_start)
    _finish(y, x_ref, gpost_ref, gnext_ref, xo_ref, ho_ref)


def _short_lhs(b_ref, cz_ref, halo_ref, cw_ref, w_ref, ext_ref, *, at_seq_start):
    tm = cz_ref.shape[0]
    ext_ref[0:SHORT_HALO, :] = jnp.where(at_seq_start, 0.0, halo_ref[...].astype(F32))
    ext_ref[SHORT_HALO:, :] = cz_ref[...].astype(F32)
    cw = cw_ref[...]
    y = cw[2:3, :] * ext_ref[pl.ds(SHORT_HALO, tm), :]
    y = y + cw[1:2, :] * ext_ref[pl.ds(SHORT_HALO - 1, tm), :]
    y = y + cw[0:1, :] * ext_ref[pl.ds(SHORT_HALO - 2, tm), :]
    return _dot((b_ref[...].astype(F32) * y).astype(BF16), w_ref[...])


def _gmlp_lhs(u_ref, v_ref, lng_ref, lnb_ref, ws_ref, sb_ref, w_ref, lhs_ref, *, at_seq_start):
    del at_seq_start
    tm, hidden = u_ref.shape
    gdim = hidden // GMLP_GROUPS
    vn = _layernorm(v_ref[...].astype(F32), lng_ref[...], lnb_ref[...]).astype(BF16)
    row = lax.broadcasted_iota(jnp.int32, (CHUNK, CHUNK), 0)
    col = lax.broadcasted_iota(jnp.int32, (CHUNK, CHUNK), 1)
    causal = col <= row
    sb = sb_ref[...]
    for g in range(GMLP_GROUPS):
        ws = jnp.where(causal, ws_ref[g], 0.0).astype(BF16)
        bias = sb[:, g:g + 1]
        cols = slice(g * gdim, (g + 1) * gdim)
        for c in range(tm // CHUNK):
            rows = slice(c * CHUNK, (c + 1) * CHUNK)
            sv = _dot(ws, vn[rows, cols]) + bias
            lhs_ref[rows, cols] = (u_ref[rows, cols].astype(F32) * sv).astype(BF16)
    return _dot(lhs_ref[...], w_ref[...])


def _conf_lhs(c_ref, lng_ref, lnb_ref, w_ref, *, at_seq_start):
    del at_seq_start
    z = _layernorm(c_ref[...], lng_ref[...], lnb_ref[...])
    return _dot((z * _sigmoid(z)).astype(BF16), w_ref[...])


def _xattn_lhs(h_ref, qk_ref, vo_ref, p_ref, *, at_seq_start):
    del at_seq_start
    scale = (h_ref.shape[1] // XA_HEADS) ** -0.5
    s = _dot(h_ref[...], qk_ref[...])
    for hh in range(XA_HEADS):
        cols = slice(hh * N_MEM, (hh + 1) * N_MEM)
        sh = s[:, cols] * scale
        e = jnp.exp(sh - jnp.max(sh, axis=-1, keepdims=True))
        p_ref[:, cols] = (e * (1.0 / jnp.sum(e, axis=-1, keepdims=True))).astype(BF16)
    return _dot(p_ref[...], vo_ref[...])


def _down_lhs(a_ref, w_ref, *, at_seq_start):
    del at_seq_start
    return _dot(a_ref[...], w_ref[...])


def kernel(x, mem, mix_norm, xa_norm, xa_wq, xa_wkv, xa_wo, ffn_norm, ffn_w_gu, ffn_w_down,
           a_w_in, a_conv_w, a_w_out, b_w_in, b_v_g, b_v_b, b_w_s, b_s_bias, b_w_out,
           c_w_in, c_conv_w, c_conv_b, c_ln_g, c_ln_b, c_w_out):
    bsz, seq, d = x.shape
    t = bsz * seq
    assert seq % ROW_TILE == 0 and ROW_TILE % CHUNK == 0 and seq % min(PROJ_ROWS, t) == 0
    d_ff = ffn_w_down.shape[1]
    tiles_per_seq = seq // ROW_TILE

    mix_norm4, xa_norm4, ffn_norm4 = (v[:, :, None, :] for v in (mix_norm, xa_norm, ffn_norm))
    rows3 = lambda v: v[:, None, :]
    b_v_g3, b_v_b3, c_conv_b3, c_ln_g3, c_ln_b3 = map(rows3, (b_v_g, b_v_b, c_conv_b, c_ln_g, c_ln_b))
    b_s_bias_t = jnp.swapaxes(b_s_bias, 1, 2)
    w_down = ffn_w_down.astype(BF16)
    a_wo, b_wo, c_wo = a_w_out.astype(BF16), b_w_out.astype(BF16), c_w_out.astype(BF16)

    x2 = x.reshape(t, d)
    kv = _memory_kv(mem.reshape(bsz * N_MEM, d), xa_norm4, xa_wkv)
    qk, vo = _fold_memory(kv, xa_wq, xa_wo, bsz)
    h = _prenorm(x2, (mix_norm4, 0, 0))

    for i in range(DEPTH):
        kind, slot = i % N_MIXERS, i // N_MIXERS
        g_post, g_next = (mix_norm4, i, 1), (xa_norm4, i, 0)
        if kind == 0:
            b, cz = _proj_call(_short_combine, h, a_w_in, slot, 3, d, 2, "short_in")
            consts = [(a_conv_w, slot), (a_wo, slot)]
            x2, h = _row_call(
                _short_lhs, [(b, d, 0), (cz, d, 0)], (cz, SHORT_HALO),
                [_resident(*c) for c in consts], [c[0] for c in consts], x2, g_post, g_next,
                [pltpu.VMEM((ROW_TILE + SHORT_HALO, d), F32)], "short_out", seq)
        elif kind == 1:
            uv = _proj_call(_gmlp_combine, h, b_w_in, slot, 1, 2 * d, 1, "gmlp_in", tn=2 * PROJ_COLS)
            consts = [(b_v_g3, slot), (b_v_b3, slot), (b_w_s, slot), (b_s_bias_t, slot), (b_wo, slot)]
            x2, h = _row_call(
                _gmlp_lhs, [(uv, d, 0), (uv, d, 1)], None,
                [_resident(*c) for c in consts], [c[0] for c in consts], x2, g_post, g_next,
                [pltpu.VMEM((ROW_TILE, d), BF16)], "gmlp_out", seq)
        else:
            conv = _conf_in_call(h, c_w_in, c_conv_w, c_conv_b3, slot, seq)
            consts = [(c_ln_g3, slot), (c_ln_b3, slot), (c_wo, slot)]
            x2, h = _row_call(
                _conf_lhs, [(conv, d, 0)], None,
                [_resident(*c) for c in consts], [c[0] for c in consts], x2, g_post, g_next,
                [], "conf_out", seq)

        mem_cols = XA_HEADS * N_MEM
        qk_spec = pl.BlockSpec((None, None, d, mem_cols), lambda r: (i, r // tiles_per_seq, 0, 0))
        vo_spec = pl.BlockSpec((None, None, mem_cols, d), lambda r: (i, r // tiles_per_seq, 0, 0))
        x2, h = _row_call(
            _xattn_lhs, [(h, d, 0)], None, [qk_spec, vo_spec], [qk, vo],
            x2, (xa_norm4, i, 1), (ffn_norm4, i, 0), [pltpu.VMEM((ROW_TILE, mem_cols), BF16)], "xattn", seq)

        a = _proj_call(_swiglu_combine, h, ffn_w_gu, i, 2, d_ff, 1, "swiglu_in")
        last = i == DEPTH - 1
        g_next = None if last else (mix_norm4, i + 1, 0)
        out = _row_call(
            _down_lhs, [(a, d_ff, 0)], None, [_resident(w_down, i)], [w_down],
            x2, (ffn_norm4, i, 1), g_next, [], "swiglu_down", seq)
        x2, h = (out, None) if last else out

    return x2.reshape(bsz, seq, d)
```

```python
import functools

import jax
import jax.numpy as jnp
from jax import lax
from jax.experimental import pallas as pl
from jax.experimental.pallas import tpu as pltpu

DEPTH = 4
N_MIXERS = 3
N_MEM = 256
SHORT_CONV = 3
CHUNK = 128
GMLP_GROUPS = 8
CONF_CONV = 31
XA_HEADS = 4
EPS = 1e-6

F32 = jnp.float32
BF16 = jnp.bfloat16

ROW_TILE = 512
DOWN_ROW_TILE = 256
PROJ_ROWS = 1024
PROJ_COLS = 512
VMEM_LIMIT_BYTES = 56 * 1024 * 1024
SHORT_HALO = 16
CONF_HALO = 32
SUBLANES = 8
LANES = 128


def _params():
    return pltpu.CompilerParams(vmem_limit_bytes=VMEM_LIMIT_BYTES)


def _resident(arr, *lead):
    shape = arr.shape[len(lead):]
    idx = tuple(lead) + (0,) * len(shape)
    return pl.BlockSpec((None,) * len(lead) + shape, lambda *_: idx, pipeline_mode=pl.Buffered(1))


def _rms(v, g):
    ms = jnp.mean(v * v, axis=-1, keepdims=True)
    return (v * lax.rsqrt(ms + EPS)) * g


def _layernorm(v, g, b):
    mu = jnp.mean(v, axis=-1, keepdims=True)
    d = v - mu
    var = jnp.mean(d * d, axis=-1, keepdims=True)
    return (d * lax.rsqrt(var + EPS)) * g + b


def _sigmoid(v):
    return 0.5 * jnp.tanh(0.5 * v) + 0.5


def _gelu_tanh(v):
    c = 0.7978845608028654
    return v * (0.5 * (1.0 + jnp.tanh(c * (v + 0.044715 * (v * v * v)))))


def _dot(a, b):
    return jnp.dot(a, b, preferred_element_type=F32)


def _finish(y, x_ref, gpost_ref, gnext_ref, xo_ref, ho_ref):
    xn = x_ref[...] + _rms(y, gpost_ref[...])
    xo_ref[...] = xn
    if ho_ref is not None:
        ho_ref[...] = _rms(xn, gnext_ref[...]).astype(BF16)


def _prenorm_body(x_ref, g_ref, h_ref):
    h_ref[...] = _rms(x_ref[...], g_ref[...]).astype(BF16)


def _prenorm(x, g):
    t, d = x.shape
    tm = min(PROJ_ROWS, t)
    return pl.pallas_call(
        _prenorm_body,
        grid=(t // tm,),
        in_specs=[pl.BlockSpec((tm, d), lambda i: (i, 0)), _resident(*g)],
        out_specs=pl.BlockSpec((tm, d), lambda i: (i, 0)),
        out_shape=jax.ShapeDtypeStruct((t, d), BF16),
        compiler_params=_params(),
        name="prenorm",
    )(x, g[0])


def _kv_body(mem_ref, g_ref, w_ref, kv_ref, memn_ref):
    @pl.when(pl.program_id(1) == 0)
    def _():
        memn_ref[...] = _rms(mem_ref[...], g_ref[...]).astype(BF16)

    kv_ref[...] = _dot(memn_ref[...], w_ref[...].astype(BF16)).astype(BF16)


def _memory_kv(mem2d, xa_norm4, wkv):
    rows, d = mem2d.shape
    depth, _, n = wkv.shape
    tn = PROJ_COLS
    return pl.pallas_call(
        _kv_body,
        grid=(depth, n // tn),
        in_specs=[
            pl.BlockSpec((rows, d), lambda l, j: (0, 0), pipeline_mode=pl.Buffered(1)),
            pl.BlockSpec((None, None, 1, d), lambda l, j: (l, 2, 0, 0)),
            pl.BlockSpec((None, d, tn), lambda l, j: (l, 0, j)),
        ],
        out_specs=pl.BlockSpec((None, rows, tn), lambda l, j: (l, 0, j)),
        out_shape=jax.ShapeDtypeStruct((depth, rows, n), BF16),
        scratch_shapes=[pltpu.VMEM((rows, d), BF16)],
        compiler_params=_params(),
        name="memory_kv",
    )(mem2d, xa_norm4, wkv)


def _fold_qk_body(wq_ref, k_ref, qk_ref):
    r = lax.dot_general(wq_ref[...].astype(BF16), k_ref[...], (((1,), (1,)), ((), ())),
                        preferred_element_type=F32)
    for b in range(qk_ref.shape[0]):
        qk_ref[b] = r[:, b * N_MEM:(b + 1) * N_MEM].astype(BF16)


def _fold_vo_body(v_ref, wo_ref, vo_ref):
    r = _dot(v_ref[...], wo_ref[...].astype(BF16))
    for b in range(vo_ref.shape[0]):
        vo_ref[b] = r[b * N_MEM:(b + 1) * N_MEM, :].astype(BF16)


def _fold_memory(kv, wq, wo, bsz):
    depth, d, _ = wq.shape
    hd = d // XA_HEADS
    nb = min(4, bsz)
    grid = (depth, XA_HEADS, bsz // nb)
    qk = pl.pallas_call(
        _fold_qk_body,
        grid=grid,
        in_specs=[
            pl.BlockSpec((None, d, hd), lambda l, h, g: (l, 0, h)),
            pl.BlockSpec((None, nb * N_MEM, hd), lambda l, h, g: (l, g, h)),
        ],
        out_specs=pl.BlockSpec((None, nb, d, N_MEM), lambda l, h, g: (l, g, 0, h)),
        out_shape=jax.ShapeDtypeStruct((depth, bsz, d, XA_HEADS * N_MEM), BF16),
        compiler_params=_params(),
        name="fold_qk",
    )(wq, kv)
    vo = pl.pallas_call(
        _fold_vo_body,
        grid=grid,
        in_specs=[
            pl.BlockSpec((None, nb * N_MEM, hd), lambda l, h, g: (l, g, XA_HEADS + h)),
            pl.BlockSpec((None, hd, d), lambda l, h, g: (l, h, 0)),
        ],
        out_specs=pl.BlockSpec((None, nb, N_MEM, d), lambda l, h, g: (l, g, h, 0)),
        out_shape=jax.ShapeDtypeStruct((depth, bsz, XA_HEADS * N_MEM, d), BF16),
        compiler_params=_params(),
        name="fold_vo",
    )(kv, wo)
    return qk, vo


def _cast_weights(w_refs, wbf_ref):
    @pl.when(pl.program_id(1) == 0)
    def _():
        for p, w_ref in enumerate(w_refs):
            wbf_ref[p] = w_ref[...].astype(BF16)


def _proj_body(h_ref, *rest, n_parts, n_outs, combine):
    w_refs, out_refs = rest[:n_parts], rest[n_parts:n_parts + n_outs]
    wbf_ref = rest[n_parts + n_outs]
    _cast_weights(w_refs, wbf_ref)
    h = h_ref[...]
    outs = combine(*[_dot(h, wbf_ref[p]) for p in range(n_parts)])
    for o_ref, o in zip(out_refs, outs):
        o_ref[...] = o.astype(BF16)


def _conf_in_body(h_ref, wa_ref, wg_ref, cw_ref, cb_ref, o_ref, wbf_ref, ext_ref, *, seq_tiles):
    _cast_weights((wa_ref, wg_ref), wbf_ref)
    tm = o_ref.shape[0]
    at_seq_start = lax.rem(pl.program_id(1), seq_tiles) == 0

    @pl.when(at_seq_start)
    def _():
        ext_ref[0:CONF_HALO, :] = jnp.zeros((CONF_HALO, ext_ref.shape[1]), F32)

    @pl.when(jnp.logical_not(at_seq_start))
    def _():
        ext_ref[0:CONF_HALO, :] = ext_ref[tm:tm + CONF_HALO, :]

    h = h_ref[...]
    ext_ref[CONF_HALO:, :] = _dot(h, wbf_ref[0]) * _sigmoid(_dot(h, wbf_ref[1]))
    _causal_conv31(ext_ref, cw_ref, cb_ref, o_ref)


def _causal_conv31(ext_ref, cw_ref, cb_ref, o_ref):
    tm, tn = o_ref.shape
    rchunk, cchunk = 128, LANES
    span = rchunk + SUBLANES
    for cc in range(tn // cchunk):
        cols = slice(cc * cchunk, (cc + 1) * cchunk)
        for rc in range(tm // rchunk):
            first = CONF_HALO + rc * rchunk - SUBLANES
            out = None
            for b in range(SUBLANES):
                p_b = None
                for a in range((CONF_CONV - 1 - b) // SUBLANES + 1):
                    k = CONF_CONV - 1 - (SUBLANES * a + b)
                    term = cw_ref[k:k + 1, cols] * ext_ref[pl.ds(first - SUBLANES * a, span), cols]
                    p_b = term if p_b is None else p_b + term
                rolled = p_b if b == 0 else pltpu.roll(p_b, b, 0)
                out = rolled if out is None else out + rolled
            o_ref[rc * rchunk:(rc + 1) * rchunk, cols] = out[SUBLANES:] + cb_ref[:, cols]


def _proj_call(combine, h, w, slot, n_parts, n_out_cols, n_outs, name, tn=PROJ_COLS):
    t, d = h.shape
    tm = min(PROJ_ROWS, t)
    nj = n_out_cols // tn
    w_specs = [
        pl.BlockSpec((None, d, tn), functools.partial(lambda j, i, p: (slot, 0, p * nj + j), p=p))
        for p in range(n_parts)
    ]
    out_spec = pl.BlockSpec((tm, tn), lambda j, i: (i, j))
    out_shape = jax.ShapeDtypeStruct((t, n_out_cols), BF16)
    return pl.pallas_call(
        functools.partial(_proj_body, n_parts=n_parts, n_outs=n_outs, combine=combine),
        grid=(nj, t // tm),
        in_specs=[pl.BlockSpec((tm, d), lambda j, i: (i, 0))] + w_specs,
        out_specs=[out_spec] * n_outs if n_outs > 1 else out_spec,
        out_shape=[out_shape] * n_outs if n_outs > 1 else out_shape,
        scratch_shapes=[pltpu.VMEM((n_parts, d, tn), BF16)],
        compiler_params=_params(),
        name=name,
    )(h, *([w] * n_parts))


def _conf_in_call(h, w, conv_w, conv_b3, slot, seq):
    t, d = h.shape
    tm, tn = min(PROJ_ROWS, t), PROJ_COLS
    nj = d // tn
    w_specs = [
        pl.BlockSpec((None, d, tn), functools.partial(lambda j, i, p: (slot, 0, p * nj + j), p=p))
        for p in range(2)
    ]
    return pl.pallas_call(
        functools.partial(_conf_in_body, seq_tiles=seq // tm),
        grid=(nj, t // tm),
        in_specs=[pl.BlockSpec((tm, d), lambda j, i: (i, 0))] + w_specs + [
            pl.BlockSpec((None, CONF_CONV, tn), lambda j, i: (slot, 0, j)),
            pl.BlockSpec((None, 1, tn), lambda j, i: (slot, 0, j)),
        ],
        out_specs=pl.BlockSpec((tm, tn), lambda j, i: (i, j)),
        out_shape=jax.ShapeDtypeStruct((t, d), F32),
        scratch_shapes=[pltpu.VMEM((2, d, tn), BF16), pltpu.VMEM((tm + CONF_HALO, tn), F32)],
        compiler_params=_params(),
        name="conf_in",
    )(h, w, w, conv_w, conv_b3)


def _short_combine(b, c, z):
    return b, c * z


def _gmlp_combine(uv):
    return (_gelu_tanh(uv),)


def _swiglu_combine(g, u):
    return ((g * _sigmoid(g)) * u,)


def _row_call(lhs_fn, row_inputs, halo, const_specs, const_args, x, g_post, g_next, scratch, name, seq,
              tm=ROW_TILE):
    t, d = x.shape
    in_specs, args = [], []
    for arr, ncols, cb in row_inputs:
        in_specs.append(pl.BlockSpec((tm, ncols), functools.partial(lambda i, cb: (i, cb), cb=cb)))
        args.append(arr)
    if halo is not None:
        arr, hr = halo
        per_tile = tm // hr
        in_specs.append(pl.BlockSpec((hr, arr.shape[1]), lambda i: (jnp.maximum(i * per_tile - 1, 0), 0)))
        args.append(arr)
    in_specs += const_specs
    args += const_args
    row_spec = pl.BlockSpec((tm, d), lambda i: (i, 0))
    in_specs += [row_spec, _resident(*g_post)]
    args += [x, g_post[0]]
    has_next = g_next is not None
    if has_next:
        in_specs.append(_resident(*g_next))
        args.append(g_next[0])
        out_specs = [row_spec, row_spec]
        out_shape = [jax.ShapeDtypeStruct((t, d), F32), jax.ShapeDtypeStruct((t, d), BF16)]
    else:
        out_specs = row_spec
        out_shape = jax.ShapeDtypeStruct((t, d), F32)
    n_lead = len(row_inputs) + (halo is not None) + len(const_args)
    body = functools.partial(_row_body, lhs_fn=lhs_fn, n_lead=n_lead, has_next=has_next, seq_tiles=seq // tm)
    return pl.pallas_call(
        body,
        grid=(t // tm,),
        in_specs=in_specs,
        out_specs=out_specs,
        out_shape=out_shape,
        scratch_shapes=scratch,
        compiler_params=_params(),
        name=name,
    )(*args)


def _row_body(*refs, lhs_fn, n_lead, has_next, seq_tiles):
    lead, rest = refs[:n_lead], refs[n_lead:]
    if has_next:
        x_ref, gpost_ref, gnext_ref, xo_ref, ho_ref = rest[:5]
        scratch = rest[5:]
    else:
        x_ref, gpost_ref, xo_ref = rest[:3]
        gnext_ref = ho_ref = None
        scratch = rest[3:]
    at_seq_start = lax.rem(pl.program_id(0), seq_tiles) == 0
    y = lhs_fn(*lead, *scratch, at_seq_start=at_seq_start)
    _finish(y, x_ref, gpost_ref, gnext_ref, xo_ref, ho_ref)


def _short_lhs(b_ref, cz_ref, halo_ref, cw_ref, w_ref, ext_ref, *, at_seq_start):
    tm = cz_ref.shape[0]
    ext_ref[0:SHORT_HALO, :] = jnp.where(at_seq_start, 0.0, halo_ref[...].astype(F32))
    ext_ref[SHORT_HALO:, :] = cz_ref[...].astype(F32)
    cw = cw_ref[...]
    y = cw[2:3, :] * ext_ref[pl.ds(SHORT_HALO, tm), :]
    y = y + cw[1:2, :] * ext_ref[pl.ds(SHORT_HALO - 1, tm), :]
    y = y + cw[0:1, :] * ext_ref[pl.ds(SHORT_HALO - 2, tm), :]
    return _dot((b_ref[...].astype(F32) * y).astype(BF16), w_ref[...])


def _gmlp_lhs(u_ref, v_ref, lng_ref, lnb_ref, ws_ref, sb_ref, w_ref, lhs_ref, *, at_seq_start):
    del at_seq_start
    tm, hidden = u_ref.shape
    gdim = hidden // GMLP_GROUPS
    vn = _layernorm(v_ref[...].astype(F32), lng_ref[...], lnb_ref[...]).astype(BF16)
    row = lax.broadcasted_iota(jnp.int32, (CHUNK, CHUNK), 0)
    col = lax.broadcasted_iota(jnp.int32, (CHUNK, CHUNK), 1)
    causal = col <= row
    sb = sb_ref[...]
    for g in range(GMLP_GROUPS):
        ws = jnp.where(causal, ws_ref[g], 0.0).astype(BF16)
        bias = sb[:, g:g + 1]
        cols = slice(g * gdim, (g + 1) * gdim)
        for c in range(tm // CHUNK):
            rows = slice(c * CHUNK, (c + 1) * CHUNK)
            sv = _dot(ws, vn[rows, cols]) + bias
            lhs_ref[rows, cols] = (u_ref[rows, cols].astype(F32) * sv).astype(BF16)
    return _dot(lhs_ref[...], w_ref[...])


def _conf_lhs(c_ref, lng_ref, lnb_ref, w_ref, *, at_seq_start):
    del at_seq_start
    z = _layernorm(c_ref[...], lng_ref[...], lnb_ref[...])
    return _dot((z * _sigmoid(z)).astype(BF16), w_ref[...])


def _xattn_lhs(h_ref, qk_ref, vo_ref, p_ref, *, at_seq_start):
    del at_seq_start
    scale = (h_ref.shape[1] // XA_HEADS) ** -0.5
    s = _dot(h_ref[...], qk_ref[...])
    for hh in range(XA_HEADS):
        cols = slice(hh * N_MEM, (hh + 1) * N_MEM)
        sh = s[:, cols] * scale
        e = jnp.exp(sh - jnp.max(sh, axis=-1, keepdims=True))
        p_ref[:, cols] = (e * (1.0 / jnp.sum(e, axis=-1, keepdims=True))).astype(BF16)
    return _dot(p_ref[...], vo_ref[...])


def _down_lhs(a_ref, w_ref, *, at_seq_start):
    del at_seq_start
    return _dot(a_ref[...], w_ref[...])


def kernel(x, mem, mix_norm, xa_norm, xa_wq, xa_wkv, xa_wo, ffn_norm, ffn_w_gu, ffn_w_down,
           a_w_in, a_conv_w, a_w_out, b_w_in, b_v_g, b_v_b, b_w_s, b_s_bias, b_w_out,
           c_w_in, c_conv_w, c_conv_b, c_ln_g, c_ln_b, c_w_out):
    bsz, seq, d = x.shape
    t = bsz * seq
    assert seq % ROW_TILE == 0 and ROW_TILE % CHUNK == 0 and seq % min(PROJ_ROWS, t) == 0
    d_ff = ffn_w_down.shape[1]
    tiles_per_seq = seq // ROW_TILE

    mix_norm4, xa_norm4, ffn_norm4 = (v[:, :, None, :] for v in (mix_norm, xa_norm, ffn_norm))
    rows3 = lambda v: v[:, None, :]
    b_v_g3, b_v_b3, c_conv_b3, c_ln_g3, c_ln_b3 = map(rows3, (b_v_g, b_v_b, c_conv_b, c_ln_g, c_ln_b))
    b_s_bias_t = jnp.swapaxes(b_s_bias, 1, 2)
    w_down = ffn_w_down.astype(BF16)
    a_wo, b_wo, c_wo = a_w_out.astype(BF16), b_w_out.astype(BF16), c_w_out.astype(BF16)

    x2 = x.reshape(t, d)
    kv = _memory_kv(mem.reshape(bsz * N_MEM, d), xa_norm4, xa_wkv)
    qk, vo = _fold_memory(kv, xa_wq, xa_wo, bsz)
    h = _prenorm(x2, (mix_norm4, 0, 0))

    for i in range(DEPTH):
        kind, slot = i % N_MIXERS, i // N_MIXERS
        g_post, g_next = (mix_norm4, i, 1), (xa_norm4, i, 0)
        if kind == 0:
            b, cz = _proj_call(_short_combine, h, a_w_in, slot, 3, d, 2, "short_in")
            consts = [(a_conv_w, slot), (a_wo, slot)]
            x2, h = _row_call(
                _short_lhs, [(b, d, 0), (cz, d, 0)], (cz, SHORT_HALO),
                [_resident(*c) for c in consts], [c[0] for c in consts], x2, g_post, g_next,
                [pltpu.VMEM((ROW_TILE + SHORT_HALO, d), F32)], "short_out", seq)
        elif kind == 1:
            uv = _proj_call(_gmlp_combine, h, b_w_in, slot, 1, 2 * d, 1, "gmlp_in", tn=2 * PROJ_COLS)
            consts = [(b_v_g3, slot), (b_v_b3, slot), (b_w_s, slot), (b_s_bias_t, slot), (b_wo, slot)]
            x2, h = _row_call(
                _gmlp_lhs, [(uv, d, 0), (uv, d, 1)], None,
                [_resident(*c) for c in consts], [c[0] for c in consts], x2, g_post, g_next,
                [pltpu.VMEM((ROW_TILE, d), BF16)], "gmlp_out", seq)
        else:
            conv = _conf_in_call(h, c_w_in, c_conv_w, c_conv_b3, slot, seq)
            consts = [(c_ln_g3, slot), (c_ln_b3, slot), (c_wo, slot)]
            x2, h = _row_call(
                _conf_lhs, [(conv, d, 0)], None,
                [_resident(*c) for c in consts], [c[0] for c in consts], x2, g_post, g_next,
                [], "conf_out", seq)

        mem_cols = XA_HEADS * N_MEM
        qk_spec = pl.BlockSpec((None, None, d, mem_cols), lambda r: (i, r // tiles_per_seq, 0, 0))
        vo_spec = pl.BlockSpec((None, None, mem_cols, d), lambda r: (i, r // tiles_per_seq, 0, 0))
        x2, h = _row_call(
            _xattn_lhs, [(h, d, 0)], None, [qk_spec, vo_spec], [qk, vo],
            x2, (xa_norm4, i, 1), (ffn_norm4, i, 0), [pltpu.VMEM((ROW_TILE, mem_cols), BF16)], "xattn", seq)

        a = _proj_call(_swiglu_combine, h, ffn_w_gu, i, 2, d_ff, 1, "swiglu_in")
        last = i == DEPTH - 1
        g_next = None if last else (mix_norm4, i + 1, 0)
        out = _row_call(
            _down_lhs, [(a, d_ff, 0)], None, [_resident(w_down, i)], [w_down],
            x2, (ffn_norm4, i, 1), g_next, [], "swiglu_down", seq, tm=DOWN_ROW_TILE)
        x2, h = (out, None) if last else out

    return x2.reshape(bsz, seq, d)
```

```python
import functools

import jax
import jax.numpy as jnp
from jax import lax
from jax.experimental import pallas as pl
from jax.experimental.pallas import tpu as pltpu

DEPTH = 4
N_MIXERS = 3
N_MEM = 256
SHORT_CONV = 3
CHUNK = 128
GMLP_GROUPS = 8
CONF_CONV = 31
XA_HEADS = 4
EPS = 1e-6

F32 = jnp.float32
BF16 = jnp.bfloat16

ROW_TILE = 512
DOWN_ROW_TILE = 512
PROJ_ROWS = 1024
PROJ_COLS = 512
VMEM_LIMIT_BYTES = 60 * 1024 * 1024
SHORT_HALO = 16
CONF_HALO = 32
SUBLANES = 8
LANES = 128


def _params():
    return pltpu.CompilerParams(vmem_limit_bytes=VMEM_LIMIT_BYTES)


def _resident(arr, *lead):
    shape = arr.shape[len(lead):]
    idx = tuple(lead) + (0,) * len(shape)
    return pl.BlockSpec((None,) * len(lead) + shape, lambda *_: idx, pipeline_mode=pl.Buffered(1))


def _rms(v, g):
    ms = jnp.mean(v * v, axis=-1, keepdims=True)
    return (v * lax.rsqrt(ms + EPS)) * g


def _layernorm(v, g, b):
    mu = jnp.mean(v, axis=-1, keepdims=True)
    d = v - mu
    var = jnp.mean(d * d, axis=-1, keepdims=True)
    return (d * lax.rsqrt(var + EPS)) * g + b


def _sigmoid(v):
    return 0.5 * jnp.tanh(0.5 * v) + 0.5


def _gelu_tanh(v):
    c = 0.7978845608028654
    return v * (0.5 * (1.0 + jnp.tanh(c * (v + 0.044715 * (v * v * v)))))


def _dot(a, b):
    return jnp.dot(a, b, preferred_element_type=F32)


def _finish(y, x_ref, gpost_ref, gnext_ref, xo_ref, ho_ref):
    xn = x_ref[...] + _rms(y, gpost_ref[...])
    xo_ref[...] = xn
    if ho_ref is not None:
        ho_ref[...] = _rms(xn, gnext_ref[...]).astype(BF16)


def _prenorm_body(x_ref, g_ref, h_ref):
    h_ref[...] = _rms(x_ref[...], g_ref[...]).astype(BF16)


def _prenorm(x, g):
    t, d = x.shape
    tm = min(PROJ_ROWS, t)
    return pl.pallas_call(
        _prenorm_body,
        grid=(t // tm,),
        in_specs=[pl.BlockSpec((tm, d), lambda i: (i, 0)), _resident(*g)],
        out_specs=pl.BlockSpec((tm, d), lambda i: (i, 0)),
        out_shape=jax.ShapeDtypeStruct((t, d), BF16),
        compiler_params=_params(),
        name="prenorm",
    )(x, g[0])


def _kv_body(mem_ref, g_ref, w_ref, kv_ref, memn_ref):
    @pl.when(pl.program_id(1) == 0)
    def _():
        memn_ref[...] = _rms(mem_ref[...], g_ref[...]).astype(BF16)

    kv_ref[...] = _dot(memn_ref[...], w_ref[...].astype(BF16)).astype(BF16)


def _memory_kv(mem2d, xa_norm4, wkv):
    rows, d = mem2d.shape
    depth, _, n = wkv.shape
    tn = PROJ_COLS
    return pl.pallas_call(
        _kv_body,
        grid=(depth, n // tn),
        in_specs=[
            pl.BlockSpec((rows, d), lambda l, j: (0, 0), pipeline_mode=pl.Buffered(1)),
            pl.BlockSpec((None, None, 1, d), lambda l, j: (l, 2, 0, 0)),
            pl.BlockSpec((None, d, tn), lambda l, j: (l, 0, j)),
        ],
        out_specs=pl.BlockSpec((None, rows, tn), lambda l, j: (l, 0, j)),
        out_shape=jax.ShapeDtypeStruct((depth, rows, n), BF16),
        scratch_shapes=[pltpu.VMEM((rows, d), BF16)],
        compiler_params=_params(),
        name="memory_kv",
    )(mem2d, xa_norm4, wkv)


def _fold_qk_body(wq_ref, k_ref, qk_ref):
    r = lax.dot_general(wq_ref[...].astype(BF16), k_ref[...], (((1,), (1,)), ((), ())),
                        preferred_element_type=F32)
    for b in range(qk_ref.shape[0]):
        qk_ref[b] = r[:, b * N_MEM:(b + 1) * N_MEM].astype(BF16)


def _fold_vo_body(v_ref, wo_ref, vo_ref):
    r = _dot(v_ref[...], wo_ref[...].astype(BF16))
    for b in range(vo_ref.shape[0]):
        vo_ref[b] = r[b * N_MEM:(b + 1) * N_MEM, :].astype(BF16)


def _fold_memory(kv, wq, wo, bsz):
    depth, d, _ = wq.shape
    hd = d // XA_HEADS
    nb = min(4, bsz)
    grid = (depth, XA_HEADS, bsz // nb)
    qk = pl.pallas_call(
        _fold_qk_body,
        grid=grid,
        in_specs=[
            pl.BlockSpec((None, d, hd), lambda l, h, g: (l, 0, h)),
            pl.BlockSpec((None, nb * N_MEM, hd), lambda l, h, g: (l, g, h)),
        ],
        out_specs=pl.BlockSpec((None, nb, d, N_MEM), lambda l, h, g: (l, g, 0, h)),
        out_shape=jax.ShapeDtypeStruct((depth, bsz, d, XA_HEADS * N_MEM), BF16),
        compiler_params=_params(),
        name="fold_qk",
    )(wq, kv)
    vo = pl.pallas_call(
        _fold_vo_body,
        grid=grid,
        in_specs=[
            pl.BlockSpec((None, nb * N_MEM, hd), lambda l, h, g: (l, g, XA_HEADS + h)),
            pl.BlockSpec((None, hd, d), lambda l, h, g: (l, h, 0)),
        ],
        out_specs=pl.BlockSpec((None, nb, N_MEM, d), lambda l, h, g: (l, g, h, 0)),
        out_shape=jax.ShapeDtypeStruct((depth, bsz, XA_HEADS * N_MEM, d), BF16),
        compiler_params=_params(),
        name="fold_vo",
    )(kv, wo)
    return qk, vo


def _cast_weights(w_refs, wbf_ref):
    @pl.when(pl.program_id(1) == 0)
    def _():
        for p, w_ref in enumerate(w_refs):
            wbf_ref[p] = w_ref[...].astype(BF16)


def _proj_body(h_ref, *rest, n_parts, n_outs, combine):
    w_refs, out_refs = rest[:n_parts], rest[n_parts:n_parts + n_outs]
    wbf_ref = rest[n_parts + n_outs]
    _cast_weights(w_refs, wbf_ref)
    h = h_ref[...]
    outs = combine(*[_dot(h, wbf_ref[p]) for p in range(n_parts)])
    for o_ref, o in zip(out_refs, outs):
        o_ref[...] = o.astype(BF16)


def _conf_in_body(h_ref, wa_ref, wg_ref, cw_ref, cb_ref, o_ref, wbf_ref, ext_ref, *, seq_tiles):
    _cast_weights((wa_ref, wg_ref), wbf_ref)
    tm = o_ref.shape[0]
    at_seq_start = lax.rem(pl.program_id(1), seq_tiles) == 0

    @pl.when(at_seq_start)
    def _():
        ext_ref[0:CONF_HALO, :] = jnp.zeros((CONF_HALO, ext_ref.shape[1]), F32)

    @pl.when(jnp.logical_not(at_seq_start))
    def _():
        ext_ref[0:CONF_HALO, :] = ext_ref[tm:tm + CONF_HALO, :]

    h = h_ref[...]
    ext_ref[CONF_HALO:, :] = _dot(h, wbf_ref[0]) * _sigmoid(_dot(h, wbf_ref[1]))
    _causal_conv31(ext_ref, cw_ref, cb_ref, o_ref)


def _causal_conv31(ext_ref, cw_ref, cb_ref, o_ref):
    tm, tn = o_ref.shape
    rchunk, cchunk = 128, LANES
    span = rchunk + SUBLANES
    for cc in range(tn // cchunk):
        cols = slice(cc * cchunk, (cc + 1) * cchunk)
        for rc in range(tm // rchunk):
            first = CONF_HALO + rc * rchunk - SUBLANES
            out = None
            for b in range(SUBLANES):
                p_b = None
                for a in range((CONF_CONV - 1 - b) // SUBLANES + 1):
                    k = CONF_CONV - 1 - (SUBLANES * a + b)
                    term = cw_ref[k:k + 1, cols] * ext_ref[pl.ds(first - SUBLANES * a, span), cols]
                    p_b = term if p_b is None else p_b + term
                rolled = p_b if b == 0 else pltpu.roll(p_b, b, 0)
                out = rolled if out is None else out + rolled
            o_ref[rc * rchunk:(rc + 1) * rchunk, cols] = out[SUBLANES:] + cb_ref[:, cols]


def _proj_call(combine, h, w, slot, n_parts, n_out_cols, n_outs, name, tn=PROJ_COLS):
    t, d = h.shape
    tm = min(PROJ_ROWS, t)
    nj = n_out_cols // tn
    w_specs = [
        pl.BlockSpec((None, d, tn), functools.partial(lambda j, i, p: (slot, 0, p * nj + j), p=p))
        for p in range(n_parts)
    ]
    out_spec = pl.BlockSpec((tm, tn), lambda j, i: (i, j))
    out_shape = jax.ShapeDtypeStruct((t, n_out_cols), BF16)
    return pl.pallas_call(
        functools.partial(_proj_body, n_parts=n_parts, n_outs=n_outs, combine=combine),
        grid=(nj, t // tm),
        in_specs=[pl.BlockSpec((tm, d), lambda j, i: (i, 0))] + w_specs,
        out_specs=[out_spec] * n_outs if n_outs > 1 else out_spec,
        out_shape=[out_shape] * n_outs if n_outs > 1 else out_shape,
        scratch_shapes=[pltpu.VMEM((n_parts, d, tn), BF16)],
        compiler_params=_params(),
        name=name,
    )(h, *([w] * n_parts))


def _conf_in_call(h, w, conv_w, conv_b3, slot, seq):
    t, d = h.shape
    tm, tn = min(PROJ_ROWS, t), PROJ_COLS
    nj = d // tn
    w_specs = [
        pl.BlockSpec((None, d, tn), functools.partial(lambda j, i, p: (slot, 0, p * nj + j), p=p))
        for p in range(2)
    ]
    return pl.pallas_call(
        functools.partial(_conf_in_body, seq_tiles=seq // tm),
        grid=(nj, t // tm),
        in_specs=[pl.BlockSpec((tm, d), lambda j, i: (i, 0))] + w_specs + [
            pl.BlockSpec((None, CONF_CONV, tn), lambda j, i: (slot, 0, j)),
            pl.BlockSpec((None, 1, tn), lambda j, i: (slot, 0, j)),
        ],
        out_specs=pl.BlockSpec((tm, tn), lambda j, i: (i, j)),
        out_shape=jax.ShapeDtypeStruct((t, d), F32),
        scratch_shapes=[pltpu.VMEM((2, d, tn), BF16), pltpu.VMEM((tm + CONF_HALO, tn), F32)],
        compiler_params=_params(),
        name="conf_in",
    )(h, w, w, conv_w, conv_b3)


def _short_combine(b, c, z):
    return b, c * z


def _gmlp_combine(uv):
    return (_gelu_tanh(uv),)


def _swiglu_combine(g, u):
    return ((g * _sigmoid(g)) * u,)


def _row_call(lhs_fn, row_inputs, halo, const_specs, const_args, x, g_post, g_next, scratch, name, seq,
              tm=ROW_TILE):
    t, d = x.shape
    in_specs, args = [], []
    for arr, ncols, cb in row_inputs:
        in_specs.append(pl.BlockSpec((tm, ncols), functools.partial(lambda i, cb: (i, cb), cb=cb)))
        args.append(arr)
    if halo is not None:
        arr, hr = halo
        per_tile = tm // hr
        in_specs.append(pl.BlockSpec((hr, arr.shape[1]), lambda i: (jnp.maximum(i * per_tile - 1, 0), 0)))
        args.append(arr)
    in_specs += const_specs
    args += const_args
    row_spec = pl.BlockSpec((tm, d), lambda i: (i, 0))
    in_specs += [row_spec, _resident(*g_post)]
    args += [x, g_post[0]]
    has_next = g_next is not None
    if has_next:
        in_specs.append(_resident(*g_next))
        args.append(g_next[0])
        out_specs = [row_spec, row_spec]
        out_shape = [jax.ShapeDtypeStruct((t, d), F32), jax.ShapeDtypeStruct((t, d), BF16)]
    else:
        out_specs = row_spec
        out_shape = jax.ShapeDtypeStruct((t, d), F32)
    n_lead = len(row_inputs) + (halo is not None) + len(const_args)
    body = functools.partial(_row_body, lhs_fn=lhs_fn, n_lead=n_lead, has_next=has_next, seq_tiles=seq // tm)
    return pl.pallas_call(
        body,
        grid=(t // tm,),
        in_specs=in_specs,
        out_specs=out_specs,
        out_shape=out_shape,
        scratch_shapes=scratch,
        compiler_params=_params(),
        name=name,
    )(*args)


def _row_body(*refs, lhs_fn, n_lead, has_next, seq_tiles):
    lead, rest = refs[:n_lead], refs[n_lead:]
    if has_next:
        x_ref, gpost_ref, gnext_ref, xo_ref, ho_ref = rest[:5]
        scratch = rest[5:]
    else:
        x_ref, gpost_ref, xo_ref = rest[:3]
        gnext_ref = ho_ref = None
        scratch = rest[3:]
    at_seq_start = lax.rem(pl.program_id(0), seq_tiles) == 0
    y = lhs_fn(*lead, *scratch, at_seq_start=at_seq_start)
    _finish(y, x_ref, gpost_ref, gnext_ref, xo_ref, ho_ref)


def _short_lhs(b_ref, cz_ref, halo_ref, cw_ref, w_ref, ext_ref, *, at_seq_start):
    tm = cz_ref.shape[0]
    ext_ref[0:SHORT_HALO, :] = jnp.where(at_seq_start, 0.0, halo_ref[...].astype(F32))
    ext_ref[SHORT_HALO:, :] = cz_ref[...].astype(F32)
    cw = cw_ref[...]
    y = cw[2:3, :] * ext_ref[pl.ds(SHORT_HALO, tm), :]
    y = y + cw[1:2, :] * ext_ref[pl.ds(SHORT_HALO - 1, tm), :]
    y = y + cw[0:1, :] * ext_ref[pl.ds(SHORT_HALO - 2, tm), :]
    return _dot((b_ref[...].astype(F32) * y).astype(BF16), w_ref[...])


def _gmlp_lhs(u_ref, v_ref, lng_ref, lnb_ref, ws_ref, sb_ref, w_ref, lhs_ref, *, at_seq_start):
    del at_seq_start
    tm, hidden = u_ref.shape
    gdim = hidden // GMLP_GROUPS
    vn = _layernorm(v_ref[...].astype(F32), lng_ref[...], lnb_ref[...]).astype(BF16)
    row = lax.broadcasted_iota(jnp.int32, (CHUNK, CHUNK), 0)
    col = lax.broadcasted_iota(jnp.int32, (CHUNK, CHUNK), 1)
    causal = col <= row
    sb = sb_ref[...]
    for g in range(GMLP_GROUPS):
        ws = jnp.where(causal, ws_ref[g], 0.0).astype(BF16)
        bias = sb[:, g:g + 1]
        cols = slice(g * gdim, (g + 1) * gdim)
        for c in range(tm // CHUNK):
            rows = slice(c * CHUNK, (c + 1) * CHUNK)
            sv = _dot(ws, vn[rows, cols]) + bias
            lhs_ref[rows, cols] = (u_ref[rows, cols].astype(F32) * sv).astype(BF16)
    return _dot(lhs_ref[...], w_ref[...])


def _conf_lhs(c_ref, lng_ref, lnb_ref, w_ref, *, at_seq_start):
    del at_seq_start
    z = _layernorm(c_ref[...], lng_ref[...], lnb_ref[...])
    return _dot((z * _sigmoid(z)).astype(BF16), w_ref[...])


def _xattn_lhs(h_ref, qk_ref, vo_ref, p_ref, *, at_seq_start):
    del at_seq_start
    scale = (h_ref.shape[1] // XA_HEADS) ** -0.5
    s = _dot(h_ref[...], qk_ref[...])
    for hh in range(XA_HEADS):
        cols = slice(hh * N_MEM, (hh + 1) * N_MEM)
        sh = s[:, cols] * scale
        e = jnp.exp(sh - jnp.max(sh, axis=-1, keepdims=True))
        p_ref[:, cols] = (e * (1.0 / jnp.sum(e, axis=-1, keepdims=True))).astype(BF16)
    return _dot(p_ref[...], vo_ref[...])


def _down_lhs(a_ref, w_ref, *, at_seq_start):
    del at_seq_start
    return _dot(a_ref[...], w_ref[...])


def kernel(x, mem, mix_norm, xa_norm, xa_wq, xa_wkv, xa_wo, ffn_norm, ffn_w_gu, ffn_w_down,
           a_w_in, a_conv_w, a_w_out, b_w_in, b_v_g, b_v_b, b_w_s, b_s_bias, b_w_out,
           c_w_in, c_conv_w, c_conv_b, c_ln_g, c_ln_b, c_w_out):
    bsz, seq, d = x.shape
    t = bsz * seq
    assert seq % ROW_TILE == 0 and ROW_TILE % CHUNK == 0 and seq % min(PROJ_ROWS, t) == 0
    d_ff = ffn_w_down.shape[1]
    tiles_per_seq = seq // ROW_TILE

    mix_norm4, xa_norm4, ffn_norm4 = (v[:, :, None, :] for v in (mix_norm, xa_norm, ffn_norm))
    rows3 = lambda v: v[:, None, :]
    b_v_g3, b_v_b3, c_conv_b3, c_ln_g3, c_ln_b3 = map(rows3, (b_v_g, b_v_b, c_conv_b, c_ln_g, c_ln_b))
    b_s_bias_t = jnp.swapaxes(b_s_bias, 1, 2)
    w_down = ffn_w_down.astype(BF16)
    a_wo, b_wo, c_wo = a_w_out.astype(BF16), b_w_out.astype(BF16), c_w_out.astype(BF16)

    x2 = x.reshape(t, d)
    kv = _memory_kv(mem.reshape(bsz * N_MEM, d), xa_norm4, xa_wkv)
    qk, vo = _fold_memory(kv, xa_wq, xa_wo, bsz)
    h = _prenorm(x2, (mix_norm4, 0, 0))

    for i in range(DEPTH):
        kind, slot = i % N_MIXERS, i // N_MIXERS
        g_post, g_next = (mix_norm4, i, 1), (xa_norm4, i, 0)
        if kind == 0:
            b, cz = _proj_call(_short_combine, h, a_w_in, slot, 3, d, 2, "short_in")
            consts = [(a_conv_w, slot), (a_wo, slot)]
            x2, h = _row_call(
                _short_lhs, [(b, d, 0), (cz, d, 0)], (cz, SHORT_HALO),
                [_resident(*c) for c in consts], [c[0] for c in consts], x2, g_post, g_next,
                [pltpu.VMEM((ROW_TILE + SHORT_HALO, d), F32)], "short_out", seq)
        elif kind == 1:
            uv = _proj_call(_gmlp_combine, h, b_w_in, slot, 1, 2 * d, 1, "gmlp_in", tn=2 * PROJ_COLS)
            consts = [(b_v_g3, slot), (b_v_b3, slot), (b_w_s, slot), (b_s_bias_t, slot), (b_wo, slot)]
            x2, h = _row_call(
                _gmlp_lhs, [(uv, d, 0), (uv, d, 1)], None,
                [_resident(*c) for c in consts], [c[0] for c in consts], x2, g_post, g_next,
                [pltpu.VMEM((ROW_TILE, d), BF16)], "gmlp_out", seq)
        else:
            conv = _conf_in_call(h, c_w_in, c_conv_w, c_conv_b3, slot, seq)
            consts = [(c_ln_g3, slot), (c_ln_b3, slot), (c_wo, slot)]
            x2, h = _row_call(
                _conf_lhs, [(conv, d, 0)], None,
                [_resident(*c) for c in consts], [c[0] for c in consts], x2, g_post, g_next,
                [], "conf_out", seq)

        mem_cols = XA_HEADS * N_MEM
        qk_spec = pl.BlockSpec((None, None, d, mem_cols), lambda r: (i, r // tiles_per_seq, 0, 0))
        vo_spec = pl.BlockSpec((None, None, mem_cols, d), lambda r: (i, r // tiles_per_seq, 0, 0))
        x2, h = _row_call(
            _xattn_lhs, [(h, d, 0)], None, [qk_spec, vo_spec], [qk, vo],
            x2, (xa_norm4, i, 1), (ffn_norm4, i, 0), [pltpu.VMEM((ROW_TILE, mem_cols), BF16)], "xattn", seq)

        a = _proj_call(_swiglu_combine, h, ffn_w_gu, i, 2, d_ff, 1, "swiglu_in")
        last = i == DEPTH - 1
        g_next = None if last else (mix_norm4, i + 1, 0)
        out = _row_call(
            _down_lhs, [(a, d_ff, 0)], None, [_resident(w_down, i)], [w_down],
            x2, (ffn_norm4, i, 1), g_next, [], "swiglu_down", seq, tm=DOWN_ROW_TILE)
        x2, h = (out, None) if last else out

    return x2.reshape(bsz, seq, d)
```
